```python
import jax, jax.numpy as jnp
from jax import lax
import numpy as np

D_MODEL = 1024
BATCH = 1
SEQ = 16384
DEPTH = 2
DEC_BATCH = 32
DEC_SEQ = 8
PAST_LEN = 16384
PAGE_SIZE = 128

MOBA_HEADS = 8
MOBA_HD = 64
MOBA_BLOCK = 256
MOBA_TOPK = 3
MOBA_QBLOCK = 128
ROPE_THETA = 500000.0
ROPE_DIMS = MOBA_HD // 4
SCONV_CH = 512
SCONV_W = 3
GDN_QK_HEADS = 8
GDN_V_HEADS = 16
GDN_DK = 128
GDN_DV = 128
GDN_CONV_W = 4
GDN_CHUNK = 64
GDN_QK_DIM = GDN_QK_HEADS * GDN_DK
GDN_V_DIM = GDN_V_HEADS * GDN_DV
GDN_QKV_DIM = 2 * GDN_QK_DIM + GDN_V_DIM
MEM_LEN = 256
MEM_HEADS = 4
MEM_HD = D_MODEL // MEM_HEADS
N_EXPERTS = 16
N_GROUPS = 4
EXPERTS_PER_GROUP = N_EXPERTS // N_GROUPS
MOE_TOP_K = 2
EXPERT_FF = 512
DN_ALPHA = (2 * DEPTH) ** 0.25
DN_BETA = (8 * DEPTH) ** -0.25
LN_EPS = 1e-5
RMS_EPS = 1e-6

kernel_name = 'moba_shortconv_gdn_hybrid_step'


def layernorm(x, g, b):
    xf = x.astype(jnp.float32)
    xc = xf - jnp.mean(xf, axis=-1, keepdims=True)
    var = jnp.mean(xc * xc, axis=-1, keepdims=True)
    return (xc * lax.rsqrt(var + LN_EPS) * g.astype(jnp.float32) + b.astype(jnp.float32)).astype(x.dtype)


def partial_rope(x, pos):
    half = ROPE_DIMS // 2
    inv_freq = ROPE_THETA ** (-2.0 * jnp.arange(half, dtype=jnp.float32) / ROPE_DIMS)
    ang = pos.astype(jnp.float32)[:, None] * inv_freq[None, :]
    cos = jnp.cos(ang)[None, :, None, :].astype(x.dtype)
    sin = jnp.sin(ang)[None, :, None, :].astype(x.dtype)
    x1, x2, rest = x[..., :half], x[..., half:ROPE_DIMS], x[..., ROPE_DIMS:]
    return jnp.concatenate([x1 * cos - x2 * sin, x2 * cos + x1 * sin, rest], axis=-1)


def causal_dwconv(u_hist, w):
    width = w.shape[0]
    t = u_hist.shape[1] - width + 1
    y = u_hist[:, 0:t] * w[0]
    for j in range(1, width):
        y = y + u_hist[:, j:j + t] * w[j]
    return y


def to_blocks(rows):
    b, l, h, d = rows.shape
    nb = -(-l // MOBA_BLOCK)
    rows = jnp.pad(rows, ((0, 0), (0, nb * MOBA_BLOCK - l), (0, 0), (0, 0)))
    return rows.reshape(b, nb, MOBA_BLOCK, h, d).transpose(0, 3, 1, 2, 4)


def moba_chunk(q, start, kb, vb, km):
    b, qn, h, d = q.shape
    nb = kb.shape[2]
    scale = d ** -0.5
    own = start // MOBA_BLOCK
    qpos = start + jnp.arange(qn, dtype=jnp.int32)
    gate = jnp.einsum('bqhd,bhnd->bhqn', q, km).astype(jnp.float32)
    gate = jnp.where(jnp.arange(nb)[None, None, None, :] < own, gate, -jnp.inf)
    n_sel = min(MOBA_TOPK, nb)
    _, sel = lax.top_k(gate, n_sel)
    valid = sel < own
    bi = jnp.arange(b)[:, None, None, None]
    hi = jnp.arange(h)[None, :, None, None]
    k_sel = kb[bi, hi, sel]
    v_sel = vb[bi, hi, sel]
    k_own = lax.dynamic_index_in_dim(kb, own, axis=2, keepdims=False)
    v_own = lax.dynamic_index_in_dim(vb, own, axis=2, keepdims=False)
    s_sel = jnp.einsum('bqhd,bhqkjd->bhqkj', q, k_sel).astype(jnp.float32) * scale
    s_sel = jnp.where(valid[..., None], s_sel, -jnp.inf).reshape(b, h, qn, n_sel * MOBA_BLOCK)
    own_pos = own * MOBA_BLOCK + jnp.arange(MOBA_BLOCK, dtype=jnp.int32)
    s_own = jnp.einsum('bqhd,bhjd->bhqj', q, k_own).astype(jnp.float32) * scale
    s_own = jnp.where(own_pos[None, None, None, :] <= qpos[None, None, :, None], s_own, -jnp.inf)
    p = jax.nn.softmax(jnp.concatenate([s_sel, s_own], axis=-1), axis=-1).astype(vb.dtype)
    p_sel = p[..., :n_sel * MOBA_BLOCK].reshape(b, h, qn, n_sel, MOBA_BLOCK)
    p_own = p[..., n_sel * MOBA_BLOCK:]
    return jnp.einsum('bhqkj,bhqkjd->bqhd', p_sel, v_sel) + jnp.einsum('bhqj,bhjd->bqhd', p_own, v_own)


def moba_attention(q, q_start, k_rows, v_rows):
    b, t, h, d = q.shape
    qblk = min(MOBA_QBLOCK, t)
    nq = t // qblk
    kb = to_blocks(k_rows)
    vb = to_blocks(v_rows)
    km = jnp.mean(kb.astype(jnp.float32), axis=3).astype(kb.dtype)
    qc = jnp.moveaxis(q.reshape(b, nq, qblk, h, d), 1, 0)
    starts = q_start + qblk * jnp.arange(nq, dtype=jnp.int32)
    o = lax.map(lambda a: moba_chunk(a[0], a[1], kb, vb, km), (qc, starts))
    return jnp.moveaxis(o, 0, 1).reshape(b, t, h, d)


def ab_mixer(x, pos0, k_past, v_past, conv_hist, w_in, sconv_w, w_out):
    b, t, _ = x.shape
    aw = MOBA_HEADS * MOBA_HD
    q, k, v, u, gate_b, gate_c = jnp.split(
        x @ w_in, [aw, 2 * aw, 3 * aw, 3 * aw + SCONV_CH, 3 * aw + 2 * SCONV_CH], axis=-1)
    pos = pos0 + jnp.arange(t, dtype=jnp.int32)
    q = partial_rope(q.reshape(b, t, MOBA_HEADS, MOBA_HD), pos)
    k = partial_rope(k.reshape(b, t, MOBA_HEADS, MOBA_HD), pos)
    v = v.reshape(b, t, MOBA_HEADS, MOBA_HD)
    k_rows = jnp.concatenate([k_past.astype(k.dtype), k], axis=1)
    v_rows = jnp.concatenate([v_past.astype(v.dtype), v], axis=1)
    attn = moba_attention(q, pos0, k_rows, v_rows).reshape(b, t, aw)
    hist = jnp.concatenate([conv_hist.astype(x.dtype), gate_c * u], axis=1)
    conv = gate_b * causal_dwconv(hist, sconv_w)
    out = jnp.concatenate([attn, conv], axis=-1) @ w_out
    return out, k, v, hist[:, -(SCONV_W - 1):]


def l2norm(x):
    xf = x.astype(jnp.float32)
    return xf * lax.rsqrt(jnp.sum(xf * xf, axis=-1, keepdims=True) + RMS_EPS)


def gated_delta_chunked(q, k, v, g, beta, s0):
    f32 = jnp.float32
    b, t, h, dk = k.shape
    dv = v.shape[-1]
    c = min(GDN_CHUNK, t)
    pad = (-t) % c
    n = (t + pad) // c

    def prep(a):
        a = a.astype(f32)
        a = jnp.pad(a, [(0, 0), (0, pad)] + [(0, 0)] * (a.ndim - 2))
        a = jnp.moveaxis(a, 2, 1)
        a = a.reshape(a.shape[:2] + (n, c) + a.shape[3:])
        return jnp.moveaxis(a, 2, 0)

    qc, kc, vc, gch, bc = prep(q), prep(k), prep(v), prep(g), prep(beta)
    gcum = jnp.cumsum(gch, axis=-1)
    idx = jnp.arange(c)
    incl = idx[:, None] >= idx[None, :]
    strict = idx[:, None] > idx[None, :]
    decay = jnp.exp(jnp.where(incl, gcum[..., :, None] - gcum[..., None, :], -jnp.inf))
    kb = kc * bc[..., None]
    vb = vc * bc[..., None]
    lmat = jnp.where(strict, jnp.einsum('nbhid,nbhjd->nbhij', kb, kc) * decay, 0.0)
    eye = jnp.eye(c, dtype=f32)
    tinv = lax.linalg.triangular_solve(eye + lmat, jnp.broadcast_to(eye, lmat.shape),
                                       left_side=True, lower=True, unit_diagonal=True)
    u = tinv @ vb
    w = tinv @ (kb * jnp.exp(gcum)[..., None])
    qk = jnp.einsum('nbhid,nbhjd->nbhij', qc, kc) * decay
    qg = qc * jnp.exp(gcum)[..., None]
    glast = gcum[..., -1]
    kdec = kc * jnp.exp(glast[..., None] - gcum)[..., None]

    def step(s, xs):
        u_n, w_n, qk_n, qg_n, kdec_n, gl_n = xs
        v_new = u_n - w_n @ s
        o_n = qg_n @ s + qk_n @ v_new
        s = s * jnp.exp(gl_n)[..., None, None] + jnp.swapaxes(kdec_n, -1, -2) @ v_new
        return s, o_n

    s_fin, o = lax.scan(step, s0.astype(f32), (u, w, qk, qg, kdec, glast))
    o = jnp.moveaxis(o, 0, 2).reshape(b, h, n * c, dv)[:, :, :t]
    return jnp.moveaxis(o, 1, 2), s_fin


def gdn_mixer(x, conv_hist, s0, w_in, conv_w, a_log, dt_bias, norm_w, w_out):
    b, t, _ = x.shape
    qkv, z, a, bb = jnp.split(
        x @ w_in, [GDN_QKV_DIM, GDN_QKV_DIM + GDN_V_DIM, GDN_QKV_DIM + GDN_V_DIM + GDN_V_HEADS], axis=-1)
    hist = jnp.concatenate([conv_hist.astype(x.dtype), qkv], axis=1)
    qkv = jax.nn.silu(causal_dwconv(hist, conv_w))
    q, k, v = jnp.split(qkv, [GDN_QK_DIM, 2 * GDN_QK_DIM], axis=-1)
    rep = GDN_V_HEADS // GDN_QK_HEADS
    q = jnp.repeat(l2norm(q.reshape(b, t, GDN_QK_HEADS, GDN_DK)), rep, axis=2) * (GDN_DK ** -0.5)
    k = jnp.repeat(l2norm(k.reshape(b, t, GDN_QK_HEADS, GDN_DK)), rep, axis=2)
    v = v.reshape(b, t, GDN_V_HEADS, GDN_DV)
    beta = jax.nn.sigmoid(bb.astype(jnp.float32))
    g = -jnp.exp(a_log.astype(jnp.float32)) * jax.nn.softplus(a.astype(jnp.float32) + dt_bias.astype(jnp.float32))
    o, s_fin = gated_delta_chunked(q, k, v, g, beta, s0)
    zf = z.reshape(b, t, GDN_V_HEADS, GDN_DV).astype(jnp.float32)
    o = o * lax.rsqrt(jnp.mean(o * o, axis=-1, keepdims=True) + RMS_EPS) * norm_w.astype(jnp.float32) * jax.nn.silu(zf)
    out = o.reshape(b, t, GDN_V_DIM).astype(x.dtype) @ w_out
    return out, s_fin, hist[:, -(GDN_CONV_W - 1):]


def mem_attention(x, mk, mv, wq, wo):
    b, t, _ = x.shape
    q = (x @ wq).reshape(b, t, MEM_HEADS, MEM_HD)
    s = jnp.einsum('bthd,bmhd->bhtm', q, mk).astype(jnp.float32) * (MEM_HD ** -0.5)
    p = jax.nn.softmax(s, axis=-1).astype(x.dtype)
    o = jnp.einsum('bhtm,bmhd->bthd', p, mv).reshape(b, t, MEM_HEADS * MEM_HD)
    return o @ wo


def moe(x, w_router, router_bias, wg, wu, wd):
    b, t, d = x.shape
    xt = x.reshape(b * t, d)
    scores = jax.nn.softmax((xt @ w_router).astype(jnp.float32), axis=-1)
    biased = scores + router_bias.astype(jnp.float32)
    grp_score = lax.top_k(biased.reshape(-1, N_GROUPS, EXPERTS_PER_GROUP), MOE_TOP_K)[0].sum(-1)
    best = jnp.argmax(grp_score, axis=-1)
    in_grp = (jnp.arange(N_EXPERTS) // EXPERTS_PER_GROUP)[None, :] == best[:, None]
    _, eidx = lax.top_k(jnp.where(in_grp, biased, -jnp.inf), MOE_TOP_K)
    wsel = jnp.take_along_axis(scores, eidx, axis=-1)
    wsel = wsel / jnp.sum(wsel, axis=-1, keepdims=True)
    combine = jnp.sum(jax.nn.one_hot(eidx, N_EXPERTS, dtype=jnp.float32) * wsel[..., None], axis=-2).astype(x.dtype)
    h = jax.nn.silu(jnp.einsum('nd,edf->nef', xt, wg)) * jnp.einsum('nd,edf->nef', xt, wu) * combine[..., None]
    return jnp.einsum('nef,efd->nd', h, wd).reshape(b, t, d)


def setup_inputs(seed: int = 0) -> dict:
    key = jax.random.key(seed)
    ks = iter(jax.random.split(key, 40))
    f32 = jnp.float32

    def nrm(shape, scale=1.0):
        return jax.random.normal(next(ks), shape, f32) * scale

    n_pages = PAST_LEN // PAGE_SIZE
    n_used = DEC_BATCH * n_pages
    n_phys = n_used + max(1, n_used // 4)
    aw = MOBA_HEADS * MOBA_HD
    ab_in = 3 * aw + 3 * SCONV_CH
    gdn_in = GDN_QKV_DIM + GDN_V_DIM + 2 * GDN_V_HEADS
    mw = MEM_HEADS * MEM_HD
    return {
        'x_prompt': nrm((BATCH, SEQ, D_MODEL)),
        'x_sample': nrm((DEC_BATCH, DEC_SEQ, D_MODEL)),
        'mem_prompt': nrm((BATCH, MEM_LEN, D_MODEL)),
        'cache_moba_k': nrm((n_phys, PAGE_SIZE, MOBA_HEADS, MOBA_HD)),
        'cache_moba_v': nrm((n_phys, PAGE_SIZE, MOBA_HEADS, MOBA_HD)),
        'page_table': jax.random.permutation(next(ks), n_phys)[:n_used].reshape(DEC_BATCH, n_pages).astype(jnp.int32),
        'state_sconv': nrm((DEC_BATCH, SCONV_W - 1, SCONV_CH)),
        'state_gdn': nrm((DEC_BATCH, GDN_V_HEADS, GDN_DK, GDN_DV), 0.5),
        'state_gdn_conv': nrm((DEC_BATCH, GDN_CONV_W - 1, GDN_QKV_DIM)),
        'cache_mem_k': nrm((DEPTH, DEC_BATCH, MEM_LEN, MEM_HEADS, MEM_HD)),
        'cache_mem_v': nrm((DEPTH, DEC_BATCH, MEM_LEN, MEM_HEADS, MEM_HD)),
        'w_in_ab': nrm((D_MODEL, ab_in), D_MODEL ** -0.5),
        'w_out_ab': nrm((aw + SCONV_CH, D_MODEL), (aw + SCONV_CH) ** -0.5 * DN_BETA),
        'sconv_w': nrm((SCONV_W, SCONV_CH), SCONV_W ** -0.5),
        'w_in_gdn': nrm((D_MODEL, gdn_in), D_MODEL ** -0.5),
        'gdn_conv_w': nrm((GDN_CONV_W, GDN_QKV_DIM), GDN_CONV_W ** -0.5),
        'gdn_a_log': jnp.log(jax.random.uniform(next(ks), (GDN_V_HEADS,), f32, 1.0, 16.0)),
        'gdn_dt_bias': nrm((GDN_V_HEADS,), 0.1),
        'gdn_norm_w': 1.0 + nrm((GDN_DV,), 0.02),
        'w_out_gdn': nrm((GDN_V_DIM, D_MODEL), GDN_V_DIM ** -0.5 * DN_BETA),
        'mem_wq': nrm((DEPTH, D_MODEL, mw), D_MODEL ** -0.5),
        'mem_wk': nrm((DEPTH, D_MODEL, mw), D_MODEL ** -0.5),
        'mem_wv': nrm((DEPTH, D_MODEL, mw), D_MODEL ** -0.5),
        'mem_wo': nrm((DEPTH, mw, D_MODEL), mw ** -0.5 * DN_BETA),
        'ln_g': 1.0 + nrm((DEPTH, 3, D_MODEL), 0.02),
        'ln_b': nrm((DEPTH, 3, D_MODEL), 0.02),
        'w_router': nrm((D_MODEL, N_EXPERTS), D_MODEL ** -0.5),
        'router_bias': nrm((N_EXPERTS,), 0.01),
        'moe_w_gate': nrm((DEPTH, N_EXPERTS, D_MODEL, EXPERT_FF), D_MODEL ** -0.5),
        'moe_w_up': nrm((DEPTH, N_EXPERTS, D_MODEL, EXPERT_FF), D_MODEL ** -0.5),
        'moe_w_down': nrm((DEPTH, N_EXPERTS, EXPERT_FF, D_MODEL), EXPERT_FF ** -0.5 * DN_BETA),
    }


def reference(x_prompt, x_sample, mem_prompt, cache_moba_k, cache_moba_v, page_table, state_sconv,
              state_gdn, state_gdn_conv, cache_mem_k, cache_mem_v, w_in_ab, w_out_ab, sconv_w, w_in_gdn,
              gdn_conv_w, gdn_a_log, gdn_dt_bias, gdn_norm_w, w_out_gdn, mem_wq, mem_wk, mem_wv, mem_wo,
              ln_g, ln_b, w_router, router_bias, moe_w_gate, moe_w_up, moe_w_down):

    def trunk(x, pos0, k_past, v_past, sconv_hist, gdn_s, gdn_hist, mem_k, mem_v):
        for l in range(DEPTH):
            if l % 2 == 0:
                h, k_new, v_new, sconv_new = ab_mixer(x, pos0, k_past, v_past, sconv_hist,
                                                      w_in_ab, sconv_w, w_out_ab)
            else:
                h, gdn_s_new, gdn_hist_new = gdn_mixer(x, gdn_hist, gdn_s, w_in_gdn, gdn_conv_w, gdn_a_log,
                                                       gdn_dt_bias, gdn_norm_w, w_out_gdn)
            x = layernorm(DN_ALPHA * x + h, ln_g[l, 0], ln_b[l, 0])
            x = layernorm(DN_ALPHA * x + mem_attention(x, mem_k[l], mem_v[l], mem_wq[l], mem_wo[l]),
                          ln_g[l, 1], ln_b[l, 1])
            x = layernorm(DN_ALPHA * x + moe(x, w_router, router_bias, moe_w_gate[l], moe_w_up[l], moe_w_down[l]),
                          ln_g[l, 2], ln_b[l, 2])
        return x, k_new, v_new, sconv_new, gdn_s_new, gdn_hist_new

    b = x_prompt.shape[0]
    dt = x_prompt.dtype
    m = mem_prompt.shape[1]
    mem_k_p = jnp.einsum('bmd,lde->lbme', mem_prompt, mem_wk).reshape(DEPTH, b, m, MEM_HEADS, MEM_HD)
    mem_v_p = jnp.einsum('bmd,lde->lbme', mem_prompt, mem_wv).reshape(DEPTH, b, m, MEM_HEADS, MEM_HD)
    y_prompt, moba_k_p, moba_v_p, sconv_p, gdn_s_p, gdn_conv_p = trunk(
        x_prompt, 0,
        jnp.zeros((b, 0, MOBA_HEADS, MOBA_HD), dt), jnp.zeros((b, 0, MOBA_HEADS, MOBA_HD), dt),
        jnp.zeros((b, SCONV_W - 1, SCONV_CH), dt),
        jnp.zeros((b, GDN_V_HEADS, GDN_DK, GDN_DV), jnp.float32),
        jnp.zeros((b, GDN_CONV_W - 1, GDN_QKV_DIM), dt),
        mem_k_p, mem_v_p)

    db = x_sample.shape[0]
    past_len = page_table.shape[1] * PAGE_SIZE
    k_past = cache_moba_k[page_table].reshape(db, past_len, MOBA_HEADS, MOBA_HD)
    v_past = cache_moba_v[page_table].reshape(db, past_len, MOBA_HEADS, MOBA_HD)
    y_sample, moba_k_s, moba_v_s, sconv_s, gdn_s_s, gdn_conv_s = trunk(
        x_sample, past_len, k_past, v_past, state_sconv, state_gdn, state_gdn_conv, cache_mem_k, cache_mem_v)

    return (y_prompt, y_sample, moba_k_p, moba_v_p, sconv_p, gdn_s_p, gdn_conv_p, mem_k_p, mem_v_p,
            moba_k_s, moba_v_s, sconv_s, gdn_s_s, gdn_conv_s)
```

```python
import functools
import math

import jax
import jax.numpy as jnp
from jax import lax
from jax.experimental import pallas as pl
from jax.experimental.pallas import tpu as pltpu

F32 = jnp.float32
BF16 = jnp.bfloat16

D_MODEL = 1024
DEPTH = 2
PAGE = 128
HEADS_A = 8
HD_A = 64
AW = HEADS_A * HD_A
KBLK = 256
TOPK_A = 3
ROPE_THETA = 500000.0
ROPE_DIMS = HD_A // 4
SCONV_CH = 512
SCONV_W = 3
G_QK_HEADS = 8
G_V_HEADS = 16
G_DK = 128
G_DV = 128
G_CONV_W = 4
G_CHUNK = 64
G_QK_DIM = G_QK_HEADS * G_DK
G_V_DIM = G_V_HEADS * G_DV
G_QKV_DIM = 2 * G_QK_DIM + G_V_DIM
MEM_LEN = 256
MEM_HEADS = 4
MEM_HD = D_MODEL // MEM_HEADS
N_EXPERTS = 16
N_GROUPS = 4
EXP_PER_GROUP = N_EXPERTS // N_GROUPS
EXPERT_FF = 512
DN_ALPHA = (2 * DEPTH) ** 0.25
LN_EPS = 1e-5
RMS_EPS = 1e-6
NEG_BIG = -1e30

SUBLANES = 8
MIB = 1024 * 1024

_NT = (((1,), (1,)), ((), ()))
_TN = (((0,), (0,)), ((), ()))


def _cp(sem, vmem_mib):
    return pltpu.CompilerParams(dimension_semantics=sem, vmem_limit_bytes=vmem_mib * MIB)


def _dot(a, b):
    return jnp.dot(a, b, preferred_element_type=F32)


def _dot_nt(a, b):
    return lax.dot_general(a, b, _NT, preferred_element_type=F32)


def _split3(a):
    a0 = a.astype(BF16)
    r = a - a0.astype(F32)
    a1 = r.astype(BF16)
    a2 = (r - a1.astype(F32)).astype(BF16)
    return a0, a1, a2


def _dot_hi(a, b):
    a0 = a.astype(BF16)
    a1 = (a - a0.astype(F32)).astype(BF16)
    b0 = b.astype(BF16)
    b1 = (b - b0.astype(F32)).astype(BF16)
    return _dot(a0, b0) + _dot(a1, b0) + _dot(a0, b1)


def _layernorm(y, g, b):
    mu = jnp.mean(y, axis=-1, keepdims=True)
    yc = y - mu
    var = jnp.mean(yc * yc, axis=-1, keepdims=True)
    return yc * lax.rsqrt(var + LN_EPS) * g + b


def _sigmoid(x):
    return 1.0 / (1.0 + jnp.exp(-x))


def _silu(x):
    return x * _sigmoid(x)


def _mm_kernel(x_ref, w_ref, o_ref, xb_ref):
    @pl.when(pl.program_id(1) == 0)
    def _():
        xb_ref[...] = x_ref[...].astype(BF16)

    o_ref[...] = _dot(xb_ref[...], w_ref[...])


def _mm(x, w, tm, tn, name):
    n, k = x.shape
    f = w.shape[1]
    return pl.pallas_call(
        _mm_kernel,
        grid=(n // tm, f // tn),
        in_specs=[pl.BlockSpec((tm, k), lambda i, j: (i, 0)),
                  pl.BlockSpec((k, tn), lambda i, j: (0, j))],
        out_specs=pl.BlockSpec((tm, tn), lambda i, j: (i, j)),
        out_shape=jax.ShapeDtypeStruct((n, f), F32),
        scratch_shapes=[pltpu.VMEM((tm, k), BF16)],
        compiler_params=_cp(("arbitrary", "arbitrary"), 48),
        name=name,
    )(x, w)


def _mm_res_ln_kernel(a_ref, w_ref, x_ref, g_ref, b_ref, o_ref):
    h = _dot(a_ref[...].astype(BF16), w_ref[...])
    o_ref[...] = _layernorm(DN_ALPHA * x_ref[...] + h, g_ref[...], b_ref[...])


def _mm_res_ln(a, w, x, g, b, tm, name):
    n, k = a.shape
    return pl.pallas_call(
        _mm_res_ln_kernel,
        grid=(n // tm,),
        in_specs=[pl.BlockSpec((tm, k), lambda i: (i, 0)),
                  pl.BlockSpec((k, D_MODEL), lambda i: (0, 0)),
                  pl.BlockSpec((tm, D_MODEL), lambda i: (i, 0)),
                  pl.BlockSpec((1, D_MODEL), lambda i: (0, 0)),
                  pl.BlockSpec((1, D_MODEL), lambda i: (0, 0))],
        out_specs=pl.BlockSpec((tm, D_MODEL), lambda i: (i, 0)),
        out_shape=jax.ShapeDtypeStruct((n, D_MODEL), F32),
        compiler_params=_cp(("arbitrary",), 48),
        name=name,
    )(a, w, x, g, b)


def _ab_in_kernel(x_ref, w_ref, c_ref, sa_ref, sb_ref,
                  qb_ref, kf_ref, kb_ref, vf_ref, vb_ref, cu_ref, gb_ref):
    y = _dot(x_ref[...].astype(BF16), w_ref[...])

    def tile4(t):
        t = t[...]
        return jnp.concatenate([t, t, t, t], axis=1)

    c, sa, sb = tile4(c_ref), tile4(sa_ref), tile4(sb_ref)

    def rope(t):
        return (t * c + pltpu.roll(t, AW - ROPE_DIMS // 2, 1) * sa
                + pltpu.roll(t, ROPE_DIMS // 2, 1) * sb)

    q = rope(y[:, 0:AW])
    k = rope(y[:, AW:2 * AW])
    v = y[:, 2 * AW:3 * AW]
    u = y[:, 3 * AW:3 * AW + SCONV_CH]
    gate_b = y[:, 3 * AW + SCONV_CH:3 * AW + 2 * SCONV_CH]
    gate_c = y[:, 3 * AW + 2 * SCONV_CH:]
    qb_ref[...] = (q * (HD_A ** -0.5)).astype(BF16)
    kf_ref[...] = k
    kb_ref[...] = k.astype(BF16)
    vf_ref[...] = v
    vb_ref[...] = v.astype(BF16)
    cu_ref[...] = gate_c * u
    gb_ref[...] = gate_b


def _ab_in(x, w, tabs, tm):
    n = x.shape[0]
    fw = w.shape[1]
    row = lambda i: (i, 0)
    half = pl.BlockSpec((tm, AW), row)
    tab = pl.BlockSpec((tm, 128), row)
    return pl.pallas_call(
        _ab_in_kernel,
        grid=(n // tm,),
        in_specs=[pl.BlockSpec((tm, D_MODEL), row),
                  pl.BlockSpec((D_MODEL, fw), lambda i: (0, 0)),
                  tab, tab, tab],
        out_specs=[half] * 7,
        out_shape=[jax.ShapeDtypeStruct((n, AW), BF16),
                   jax.ShapeDtypeStruct((n, AW), F32),
                   jax.ShapeDtypeStruct((n, AW), BF16),
                   jax.ShapeDtypeStruct((n, AW), F32),
                   jax.ShapeDtypeStruct((n, AW), BF16),
                   jax.ShapeDtypeStruct((n, AW), F32),
                   jax.ShapeDtypeStruct((n, AW), F32)],
        compiler_params=_cp(("arbitrary",), 48),
        name="ab_in",
    )(x, w, *tabs)


def _rope_tables(pos):
    half = ROPE_DIMS // 2
    inv_freq = ROPE_THETA ** (-2.0 * jnp.arange(half, dtype=F32) / ROPE_DIMS)
    ang = pos.astype(F32)[:, None] * inv_freq[None, :]
    cos, sin = jnp.cos(ang), jnp.sin(ang)
    n = pos.shape[0]
    rest1 = jnp.ones((n, HD_A - ROPE_DIMS), F32)
    rest0 = jnp.zeros((n, HD_A - ROPE_DIMS), F32)
    z = jnp.zeros((n, half), F32)
    c = jnp.concatenate([cos, cos, rest1], axis=1)
    sa = jnp.concatenate([-sin, z, rest0], axis=1)
    sb = jnp.concatenate([z, sin, rest0], axis=1)
    return tuple(jnp.concatenate([t, t], axis=1) for t in (c, sa, sb))


def _block_mean_kernel(k_ref, o_ref):
    o_ref[0] = jnp.sum(k_ref[...], axis=0, keepdims=True) * (1.0 / KBLK)


def _block_mean(k):
    n = k.shape[0]
    nb = n // KBLK
    out = pl.pallas_call(
        _block_mean_kernel,
        grid=(nb,),
        in_specs=[pl.BlockSpec((KBLK, AW), lambda i: (i, 0))],
        out_specs=pl.BlockSpec((1, 1, AW), lambda i: (i, 0, 0)),
        out_shape=jax.ShapeDtypeStruct((nb, 1, AW), F32),
        compiler_params=_cp(("arbitrary",), 32),
        name="moba_block_mean",
    )(k)
    return out.reshape(nb, AW)


def _top3_mask(g, idx, n_idx, axis):
    sel = jnp.zeros(g.shape, jnp.bool_)
    for _ in range(TOPK_A):
        mx = jnp.max(g, axis=axis, keepdims=True)
        first = jnp.min(jnp.where(g == mx, idx, n_idx), axis=axis, keepdims=True)
        pick = idx == first
        sel = jnp.logical_or(sel, pick)
        g = jnp.where(pick, -jnp.inf, g)
    return sel


def _moba_prompt_kernel(q_ref, k_ref, vt_ref, km_ref, o_ref, sel_ref):
    i = pl.program_id(1)
    nb = km_ref.shape[0]
    q = q_ref[...]
    km = km_ref[...].astype(BF16)
    lane = lax.broadcasted_iota(jnp.int32, q.shape, 1)
    blk = lax.broadcasted_iota(jnp.int32, (nb, KBLK), 0)
    kpos = lax.broadcasted_iota(jnp.int32, (KBLK, KBLK), 0)
    qpos = lax.broadcasted_iota(jnp.int32, (KBLK, KBLK), 1)
    outs = []
    for h in range(2):
        qh = jnp.where(lane // HD_A == h, q, jnp.zeros_like(q))
        gate = _dot_nt(km, qh)
        gate = jnp.where(blk < i, gate, -jnp.inf)
        sel = jnp.logical_and(_top3_mask(gate, blk, nb, 0), blk < i)
        sel_ref[h] = sel.astype(F32)

        def vt_blk(j):
            return vt_ref[j, h * HD_A:(h + 1) * HD_A, :]

        kj = k_ref[pl.ds(pl.multiple_of(i * KBLK, KBLK), KBLK), :]
        s = _dot_nt(kj, qh)
        s = jnp.where(kpos <= qpos, s, NEG_BIG)
        m = jnp.max(s, axis=0, keepdims=True)
        p = jnp.exp(s - m)
        l = jnp.sum(p, axis=0, keepdims=True)
        acc = _dot(vt_blk(i), p.astype(BF16))

        def body(j, carry):
            m, l, acc = carry
            kj = k_ref[pl.ds(pl.multiple_of(j * KBLK, KBLK), KBLK), :]
            s = _dot_nt(kj, qh)
            msk = sel_ref[h, pl.ds(j, 1), :]
            s = jnp.where(msk > 0.0, s, NEG_BIG)
            m_new = jnp.maximum(m, jnp.max(s, axis=0, keepdims=True))
            a = jnp.exp(m - m_new)
            p = jnp.exp(s - m_new)
            l = a * l + jnp.sum(p, axis=0, keepdims=True)
            acc = a * acc + _dot(vt_blk(j), p.astype(BF16))
            return m_new, l, acc

        m, l, acc = lax.fori_loop(0, i, body, (m, l, acc))
        outs.append(acc / l)
    o_ref[...] = jnp.concatenate(outs, axis=0).T


def _moba_prompt(qb, kb, vb, km):
    n = qb.shape[0]
    nb = n // KBLK
    pairs = AW // 128
    vt = vb.reshape(nb, KBLK, AW).transpose(0, 2, 1)
    return pl.pallas_call(
        _moba_prompt_kernel,
        grid=(pairs, nb),
        in_specs=[pl.BlockSpec((KBLK, 128), lambda p, i: (i, p)),
                  pl.BlockSpec((n, 128), lambda p, i: (0, p)),
                  pl.BlockSpec((nb, 128, KBLK), lambda p, i: (0, p, 0)),
                  pl.BlockSpec((nb, 128), lambda p, i: (0, p))],
        out_specs=pl.BlockSpec((KBLK, 128), lambda p, i: (i, p)),
        out_shape=jax.ShapeDtypeStruct((n, AW), F32),
        scratch_shapes=[pltpu.VMEM((2, nb, KBLK), F32)],
        compiler_params=_cp(("arbitrary", "arbitrary"), 48),
        name="moba_prompt",
    )(qb, kb, vt, km)


def _moba_sample_kernel(pt_ref, q_ref, kn_ref, vn_ref, ka_ref, kb_ref, va_ref, vb_ref, o_ref,
                        qexp_ref, km_ref, s_ref, acc_ref, l_ref, *, nblk, tq):
    del pt_ref
    j = pl.program_id(1)
    rows = HEADS_A * tq

    @pl.when(j == 0)
    def _():
        q = q_ref[...]
        lane = lax.broadcasted_iota(jnp.int32, q.shape, 1)
        qexp_ref[...] = jnp.concatenate(
            [jnp.where(lane // HD_A == h, q, jnp.zeros_like(q)) for h in range(HEADS_A)], axis=0)

    @pl.when(j < nblk)
    def _():
        kblk = jnp.concatenate([ka_ref[0], kb_ref[0]], axis=0)
        km_ref[pl.ds(j, 1), :] = jnp.sum(kblk, axis=0, keepdims=True) * (1.0 / KBLK)
        s_ref[j] = _dot_nt(qexp_ref[...], kblk.astype(BF16))

    @pl.when(j == nblk - 1)
    def _():
        qe = qexp_ref[...]
        gate = _dot_nt(qe, km_ref[...].astype(BF16))
        bl = lax.broadcasted_iota(jnp.int32, gate.shape, 1)
        self_f = _top3_mask(gate, bl, nblk, 1).astype(F32)

        def col(jj):
            return jnp.sum(jnp.where(bl == jj, self_f, 0.0), axis=1, keepdims=True)

        s_own = _dot_nt(qe, kn_ref[0])
        rq = lax.broadcasted_iota(jnp.int32, s_own.shape, 0) % tq
        kt = lax.broadcasted_iota(jnp.int32, s_own.shape, 1)
        s_own = jnp.where(kt <= rq, s_own, NEG_BIG)
        m = jnp.max(s_own, axis=1, keepdims=True)

        def mbody(jj, m):
            sj = jnp.where(col(jj) > 0.0, s_ref[jj], NEG_BIG)
            return jnp.maximum(m, jnp.max(sj, axis=1, keepdims=True))

        m = lax.fori_loop(0, nblk, mbody, m)
        p_own = jnp.exp(s_own - m)
        l = jnp.sum(p_own, axis=1, keepdims=True)

        def pbody(jj, l):
            pj = jnp.where(col(jj) > 0.0, jnp.exp(s_ref[jj] - m), 0.0)
            s_ref[jj] = pj
            return l + jnp.sum(pj, axis=1, keepdims=True)

        l = lax.fori_loop(0, nblk, pbody, l)
        l_ref[...] = jnp.broadcast_to(l, l_ref.shape)
        acc_ref[...] = _dot(p_own.astype(BF16), vn_ref[0])

    @pl.when(j >= nblk)
    def _():
        vblk = jnp.concatenate([va_ref[0], vb_ref[0]], axis=0).astype(BF16)
        acc_ref[...] += _dot(s_ref[j - nblk].astype(BF16), vblk)

    @pl.when(j == 2 * nblk - 1)
    def _():
        o = acc_ref[...] / l_ref[:, 0:1]
        lane = lax.broadcasted_iota(jnp.int32, (tq, AW), 1)
        res = jnp.zeros((tq, AW), F32)
        for h in range(HEADS_A):
            res = res + jnp.where(lane // HD_A == h, o[h * tq:(h + 1) * tq, :], 0.0)
        o_ref[...] = res


def _moba_sample(qb, kb_new, vb_new, cache_k, cache_v, page_table, tq):
    n = qb.shape[0]
    nseq = n // tq
    npages = page_table.shape[1]
    nblk = npages * PAGE // KBLK
    assert KBLK == 2 * PAGE and tq <= 128
    ck = cache_k.reshape(cache_k.shape[0], PAGE, AW)
    cv = cache_v.reshape(cache_v.shape[0], PAGE, AW)
    pad = ((0, 0), (0, 128 - tq), (0, 0))
    kn = jnp.pad(kb_new.reshape(nseq, tq, AW), pad)
    vn = jnp.pad(vb_new.reshape(nseq, tq, AW), pad)
    rows = HEADS_A * tq

    def kpage(off):
        return pl.BlockSpec((1, PAGE, AW),
                            lambda b, j, pt: (pt[b, 2 * jnp.minimum(j, nblk - 1) + off], 0, 0))

    def vpage(off):
        return pl.BlockSpec((1, PAGE, AW),
                            lambda b, j, pt: (pt[b, 2 * jnp.maximum(j - nblk, 0) + off], 0, 0))

    seq3 = pl.BlockSpec((1, 128, AW), lambda b, j, pt: (b, 0, 0))
    grid_spec = pltpu.PrefetchScalarGridSpec(
        num_scalar_prefetch=1,
        grid=(nseq, 2 * nblk),
        in_specs=[pl.BlockSpec((tq, AW), lambda b, j, pt: (b, 0)), seq3, seq3,
                  kpage(0), kpage(1), vpage(0), vpage(1)],
        out_specs=pl.BlockSpec((tq, AW), lambda b, j, pt: (b, 0)),
        scratch_shapes=[pltpu.VMEM((rows, AW), BF16),
                        pltpu.VMEM((nblk, AW), F32),
                        pltpu.VMEM((nblk, rows, KBLK), F32),
                        pltpu.VMEM((rows, AW), F32),
                        pltpu.VMEM((rows, 128), F32)])
    return pl.pallas_call(
        functools.partial(_moba_sample_kernel, nblk=nblk, tq=tq),
        grid_spec=grid_spec,
        out_shape=jax.ShapeDtypeStruct((n, AW), F32),
        compiler_params=_cp(("arbitrary", "arbitrary"), 32),
        name="moba_sample",
    )(page_table, qb, kn, vn, ck, ck, cv, cv)


def _ab_out_kernel(attn_ref, cu_ref, cup_ref, hist_ref, gb_ref, x_ref, w_ref, cw_ref, g_ref, b_ref,
                   o_ref, hbuf_ref, *, tt):
    j = pl.program_id(1)
    hbuf_ref[0:SUBLANES, :] = jnp.where(j == 0, hist_ref[0], cup_ref[...])
    hbuf_ref[SUBLANES:, :] = cu_ref[...]
    cw = cw_ref[...]
    conv = jnp.zeros((tt, SCONV_CH), F32)
    for tap in range(SCONV_W):
        r0 = SUBLANES - (SCONV_W - 1) + tap
        conv = conv + hbuf_ref[r0:r0 + tt, :] * cw[tap:tap + 1, :]
    conv = gb_ref[...] * conv
    mix = jnp.concatenate([attn_ref[...], conv], axis=1).astype(BF16)
    h = _dot(mix, w_ref[...])
    o_ref[...] = _layernorm(DN_ALPHA * x_ref[...] + h, g_ref[...], b_ref[...])


def _ab_out(attn, cu, hist, gb, x, w, cw, g, b, nseq, tt):
    n = x.shape[0]
    t = n // nseq
    nt = t // tt
    hist8 = jnp.pad(hist, ((0, 0), (SUBLANES - hist.shape[1], 0), (0, 0)))
    row = lambda s, j: (s * nt + j, 0)
    prev = lambda s, j: (jnp.maximum((s * t + j * tt) // SUBLANES - 1, 0), 0)
    const = lambda s, j: (0, 0)
    return pl.pallas_call(
        functools.partial(_ab_out_kernel, tt=tt),
        grid=(nseq, nt),
        in_specs=[pl.BlockSpec((tt, AW), row),
                  pl.BlockSpec((tt, SCONV_CH), row),
                  pl.BlockSpec((SUBLANES, SCONV_CH), prev),
                  pl.BlockSpec((1, SUBLANES, SCONV_CH), lambda s, j: (s, 0, 0)),
                  pl.BlockSpec((tt, SCONV_CH), row),
                  pl.BlockSpec((tt, D_MODEL), row),
                  pl.BlockSpec((AW + SCONV_CH, D_MODEL), const),
                  pl.BlockSpec((SCONV_W, SCONV_CH), const),
                  pl.BlockSpec((1, D_MODEL), const),
                  pl.BlockSpec((1, D_MODEL), const)],
        out_specs=pl.BlockSpec((tt, D_MODEL), row),
        out_shape=jax.ShapeDtypeStruct((n, D_MODEL), F32),
        scratch_shapes=[pltpu.VMEM((tt + SUBLANES, SCONV_CH), F32)],
        compiler_params=_cp(("arbitrary", "arbitrary"), 48),
        name="ab_out",
    )(attn, cu, cu, hist8, gb, x, w, cw, g, b)


def _memattn_kernel(x_ref, wq_ref, mk_ref, mv_ref, wo_ref, g_ref, b_ref, o_ref):
    x = x_ref[...]
    q = _dot(x.astype(BF16), wq_ref[...])
    mk = mk_ref[0].astype(BF16)
    mv = mv_ref[0].astype(BF16)
    outs = []
    for h in range(MEM_HEADS):
        sl = slice(h * MEM_HD, (h + 1) * MEM_HD)
        s = _dot_nt(q[:, sl].astype(BF16), mk[:, sl]) * (MEM_HD ** -0.5)
        s = s - jnp.max(s, axis=-1, keepdims=True)
        p = jnp.exp(s)
        p = p / jnp.sum(p, axis=-1, keepdims=True)
        outs.append(_dot(p.astype(BF16), mv[:, sl]))
    o = jnp.concatenate(outs, axis=1).astype(BF16)
    y = _dot(o, wo_ref[...])
    o_ref[...] = _layernorm(DN_ALPHA * x + y, g_ref[...], b_ref[...])


def _memattn(x, wq, mk, mv, wo, g, b, tm):
    n = x.shape[0]
    nb = mk.shape[0]
    per = n // tm // nb
    row = lambda i: (i, 0)
    const = lambda i: (0, 0)
    mem = pl.BlockSpec((1, MEM_LEN, D_MODEL), lambda i: (i // per, 0, 0))
    return pl.pallas_call(
        _memattn_kernel,
        grid=(n // tm,),
        in_specs=[pl.BlockSpec((tm, D_MODEL), row),
                  pl.BlockSpec((D_MODEL, D_MODEL), const),
                  mem, mem,
                  pl.BlockSpec((D_MODEL, D_MODEL), const),
                  pl.BlockSpec((1, D_MODEL), const),
                  pl.BlockSpec((1, D_MODEL), const)],
        out_specs=pl.BlockSpec((tm, D_MODEL), row),
        out_shape=jax.ShapeDtypeStruct((n, D_MODEL), F32),
        compiler_params=_cp(("arbitrary",), 48),
        name="memattn",
    )(x, wq, mk, mv, wo, g, b)


def _route(logits, bias):
    lane = lax.broadcasted_iota(jnp.int32, logits.shape, 1)
    grp = lane // EXP_PER_GROUP
    ex = jnp.exp(logits - jnp.max(logits, axis=1, keepdims=True))
    scores = ex / jnp.sum(ex, axis=1, keepdims=True)
    biased = scores + bias

    def top2(vals):
        t1 = jnp.max(vals, axis=1, keepdims=True)
        i1 = jnp.min(jnp.where(vals == t1, lane, N_EXPERTS), axis=1, keepdims=True)
        rest = jnp.where(lane == i1, -jnp.inf, vals)
        t2 = jnp.max(rest, axis=1, keepdims=True)
        i2 = jnp.min(jnp.where(rest == t2, lane, N_EXPERTS), axis=1, keepdims=True)
        return t1, i1, t2, i2

    best = jnp.zeros((logits.shape[0], 1), jnp.int32)
    best_v = None
    for gi in range(N_GROUPS):
        t1, _, t2, _ = top2(jnp.where(grp == gi, biased, -jnp.inf))
        gs = t1 + t2
        if best_v is None:
            best_v = gs
        else:
            better = gs > best_v
            best = jnp.where(better, gi, best)
            best_v = jnp.where(better, gs, best_v)
    _, i1, _, i2 = top2(jnp.where(grp == best, biased, -jnp.inf))
    s1 = jnp.sum(jnp.where(lane == i1, scores, 0.0), axis=1, keepdims=True)
    s2 = jnp.sum(jnp.where(lane == i2, scores, 0.0), axis=1, keepdims=True)
    den = s1 + s2
    return jnp.where(lane == i1, s1 / den, 0.0) + jnp.where(lane == i2, s2 / den, 0.0)


def _moe_kernel(x_ref, wr_ref, rb_ref, wgu_ref, wd_ref, g_ref, b_ref, o_ref,
                xb_ref, comb_ref, acc_ref):
    e = pl.program_id(1)

    @pl.when(e == 0)
    def _():
        x = x_ref[...]
        xb_ref[...] = x.astype(BF16)
        comb_ref[...] = _route(_dot_hi(x, wr_ref[...]), rb_ref[...])
        acc_ref[...] = jnp.zeros_like(acc_ref)

    hgu = _dot(xb_ref[...], wgu_ref[0])
    comb = comb_ref[...]
    lane = lax.broadcasted_iota(jnp.int32, comb.shape, 1)
    ce = jnp.sum(jnp.where(lane == e, comb, 0.0), axis=1, keepdims=True)
    h = _silu(hgu[:, :EXPERT_FF]) * hgu[:, EXPERT_FF:] * ce
    acc_ref[...] += _dot(h.astype(BF16), wd_ref[0])

    @pl.when(e == N_EXPERTS - 1)
    def _():
        o_ref[...] = _layernorm(DN_ALPHA * x_ref[...] + acc_ref[...], g_ref[...], b_ref[...])


def _moe(x, wr, rb, wgu, wd, g, b, tm):
    n = x.shape[0]
    row = lambda i, e: (i, 0)
    const = lambda i, e: (0, 0)
    return pl.pallas_call(
        _moe_kernel,
        grid=(n // tm, N_EXPERTS),
        in_specs=[pl.BlockSpec((tm, D_MODEL), row),
                  pl.BlockSpec((D_MODEL, N_EXPERTS), const),
                  pl.BlockSpec((1, N_EXPERTS), const),
                  pl.BlockSpec((1, D_MODEL, 2 * EXPERT_FF), lambda i, e: (e, 0, 0)),
                  pl.BlockSpec((1, EXPERT_FF, D_MODEL), lambda i, e: (e, 0, 0)),
                  pl.BlockSpec((1, D_MODEL), const),
                  pl.BlockSpec((1, D_MODEL), const)],
        out_specs=pl.BlockSpec((tm, D_MODEL), row),
        out_shape=jax.ShapeDtypeStruct((n, D_MODEL), F32),
        scratch_shapes=[pltpu.VMEM((tm, D_MODEL), BF16),
                        pltpu.VMEM((tm, N_EXPERTS), F32),
                        pltpu.VMEM((tm, D_MODEL), F32)],
        compiler_params=_cp(("arbitrary", "arbitrary"), 48),
        name="moe",
    )(x, wr, rb, wgu, wd, g, b)


def _gdn_prep_kernel(qkv_ref, prev_ref, hist_ref, cw_ref, x_ref, wa_ref, wb_ref, alog_ref, dtb_ref,
                     qn_ref, kn_ref, v_ref, g_ref, beta_ref, hbuf_ref, *, tt):
    j = pl.program_id(1)
    hbuf_ref[0:SUBLANES, :] = jnp.where(j == 0, hist_ref[0], prev_ref[...])
    hbuf_ref[SUBLANES:, :] = qkv_ref[...]
    for c in range(G_QKV_DIM // 128):
        sl = slice(c * 128, (c + 1) * 128)
        y = jnp.zeros((tt, 128), F32)
        for tap in range(G_CONV_W):
            r0 = SUBLANES - (G_CONV_W - 1) + tap
            y = y + hbuf_ref[r0:r0 + tt, sl] * cw_ref[tap:tap + 1, sl]
        y = _silu(y)
        if c < 2 * G_QK_HEADS:
            y = y * lax.rsqrt(jnp.sum(y * y, axis=-1, keepdims=True) + RMS_EPS)
            if c < G_QK_HEADS:
                qn_ref[:, sl] = y * (G_DK ** -0.5)
            else:
                kn_ref[:, (c - G_QK_HEADS) * 128:(c - G_QK_HEADS + 1) * 128] = y
        else:
            v_ref[:, (c - 2 * G_QK_HEADS) * 128:(c - 2 * G_QK_HEADS + 1) * 128] = y
    x = x_ref[...]
    a = _dot_hi(x, wa_ref[...]) + dtb_ref[...]
    softplus = jnp.maximum(a, 0.0) + jnp.log(1.0 + jnp.exp(-jnp.abs(a)))
    g_ref[...] = -jnp.exp(alog_ref[...]) * softplus
    beta_ref[...] = _sigmoid(_dot_hi(x, wb_ref[...]))


def _gdn_prep(big, hist, cw, x, wa, wb, alog, dtb, nseq, tt):
    n = x.shape[0]
    t = n // nseq
    nt = t // tt
    hist8 = jnp.pad(hist, ((0, 0), (SUBLANES - hist.shape[1], 0), (0, 0)))
    row = lambda s, j: (s * nt + j, 0)
    prev = lambda s, j: (jnp.maximum((s * t + j * tt) // SUBLANES - 1, 0), 0)
    const = lambda s, j: (0, 0)
    hv = G_V_HEADS
    return pl.pallas_call(
        functools.partial(_gdn_prep_kernel, tt=tt),
        grid=(nseq, nt),
        in_specs=[pl.BlockSpec((tt, G_QKV_DIM), row),
                  pl.BlockSpec((SUBLANES, G_QKV_DIM), prev),
                  pl.BlockSpec((1, SUBLANES, G_QKV_DIM), lambda s, j: (s, 0, 0)),
                  pl.BlockSpec((G_CONV_W, G_QKV_DIM), const),
                  pl.BlockSpec((tt, D_MODEL), row),
                  pl.BlockSpec((D_MODEL, hv), const),
                  pl.BlockSpec((D_MODEL, hv), const),
                  pl.BlockSpec((1, hv), const),
                  pl.BlockSpec((1, hv), const)],
        out_specs=[pl.BlockSpec((tt, G_QK_DIM), row),
                   pl.BlockSpec((tt, G_QK_DIM), row),
                   pl.BlockSpec((tt, G_V_DIM), row),
                   pl.BlockSpec((tt, hv), row),
                   pl.BlockSpec((tt, hv), row)],
        out_shape=[jax.ShapeDtypeStruct((n, G_QK_DIM), F32),
                   jax.ShapeDtypeStruct((n, G_QK_DIM), F32),
                   jax.ShapeDtypeStruct((n, G_V_DIM), F32),
                   jax.ShapeDtypeStruct((n, hv), F32),
                   jax.ShapeDtypeStruct((n, hv), F32)],
        scratch_shapes=[pltpu.VMEM((tt + SUBLANES, G_QKV_DIM), F32)],
        compiler_params=_cp(("arbitrary", "arbitrary"), 48),
        name="gdn_prep",
    )(big, big, hist8, cw, x, wa, wb, alog, dtb)


def _gdn_scan_kernel(q_ref, k_ref, v_ref, z_ref, g_ref, beta_ref, gt_ref, s0_ref, nw_ref,
                     o_ref, sfin_ref, s_scr, *, c, nchunks):
    j = pl.program_id(1)

    @pl.when(j == 0)
    def _():
        s_scr[...] = s0_ref[0]

    ri = lax.broadcasted_iota(jnp.int32, (c, c), 0)
    ci = lax.broadcasted_iota(jnp.int32, (c, c), 1)
    incl = ri >= ci
    strict = ri > ci
    eye = (ri == ci).astype(F32)
    tri = incl.astype(BF16)
    tri_t = (ri <= ci).astype(BF16)
    g0, g1, g2 = _split3(g_ref[...])
    gcum = _dot(tri, g0) + _dot(tri, g1) + _dot(tri, g2)
    t0, t1, t2 = _split3(gt_ref[0])
    gcum_t = _dot(t0, tri_t) + _dot(t1, tri_t) + _dot(t2, tri_t)
    beta = beta_ref[...]
    nw = nw_ref[...]
    for h in range(G_V_HEADS):
        hq = h // (G_V_HEADS // G_QK_HEADS)
        q = q_ref[:, hq * G_DK:(hq + 1) * G_DK]
        k = k_ref[:, hq * G_DK:(hq + 1) * G_DK]
        v = v_ref[:, h * G_DV:(h + 1) * G_DV]
        gc = gcum[:, h:h + 1]
        gr = gcum_t[h:h + 1, :]
        gl = gcum[c - 1:c, h:h + 1]
        bt = beta[:, h:h + 1]
        decay = jnp.exp(jnp.where(incl, gc - gr, -jnp.inf))
        eg = jnp.exp(gc)
        kb = k * bt
        x = -jnp.where(strict, _dot_nt(kb, k) * decay, 0.0)
        tinv = eye + x
        for _ in range(int(math.log2(c)) - 1):
            x = _dot(x, x)
            tinv = tinv + _dot(tinv, x)
        u = _dot(tinv, v * bt)
        w = _dot(tinv, kb * eg)
        qk = _dot_nt(q, k) * decay
        s = s_scr[h]
        v_new = u - _dot(w, s)
        o = _dot(q * eg, s) + _dot(qk, v_new)
        kdec = k * jnp.exp(gl - gc)
        s_scr[h] = s * jnp.exp(gl) + lax.dot_general(kdec, v_new, _TN, preferred_element_type=F32)
        zf = z_ref[:, h * G_DV:(h + 1) * G_DV]
        o = o * lax.rsqrt(jnp.mean(o * o, axis=-1, keepdims=True) + RMS_EPS) * nw * _silu(zf)
        o_ref[:, h * G_DV:(h + 1) * G_DV] = o

    @pl.when(j == nchunks - 1)
    def _():
        sfin_ref[0] = s_scr[...]


def _gdn_scan(qn, kn, v, big, g, beta, s0, nw, nseq, c):
    n = qn.shape[0]
    t = n // nseq
    nchunks = t // c
    hv = G_V_HEADS
    gt = g.reshape(nseq * nchunks, c, hv).transpose(0, 2, 1)
    row = lambda s, j: (s * nchunks + j, 0)
    zcol = G_QKV_DIM // G_V_DIM
    return pl.pallas_call(
        functools.partial(_gdn_scan_kernel, c=c, nchunks=nchunks),
        grid=(nseq, nchunks),
        in_specs=[pl.BlockSpec((c, G_QK_DIM), row),
                  pl.BlockSpec((c, G_QK_DIM), row),
                  pl.BlockSpec((c, G_V_DIM), row),
                  pl.BlockSpec((c, G_V_DIM), lambda s, j: (s * nchunks + j, zcol)),
                  pl.BlockSpec((c, hv), row),
                  pl.BlockSpec((c, hv), row),
                  pl.BlockSpec((1, hv, c), lambda s, j: (s * nchunks + j, 0, 0)),
                  pl.BlockSpec((1, hv, G_DK, G_DV), lambda s, j: (s, 0, 0, 0)),
                  pl.BlockSpec((1, G_DV), lambda s, j: (0, 0))],
        out_specs=[pl.BlockSpec((c, G_V_DIM), row),
                   pl.BlockSpec((1, hv, G_DK, G_DV), lambda s, j: (s, 0, 0, 0))],
        out_shape=[jax.ShapeDtypeStruct((n, G_V_DIM), F32),
                   jax.ShapeDtypeStruct((nseq, hv, G_DK, G_DV), F32)],
        scratch_shapes=[pltpu.VMEM((hv, G_DK, G_DV), F32)],
        compiler_params=_cp(("arbitrary", "arbitrary"), 48),
        name="gdn_scan",
    )(qn, kn, v, big, g, beta, gt, s0, nw)


def _trunk(x, nseq, pos, moba_fn, sconv_hist, gdn_s0, gdn_hist, mem_k, mem_v, wts, tiles):
    n = x.shape[0]
    t = n // nseq
    assert t >= G_CONV_W - 1 and t >= SCONV_W - 1
    row = lambda a: a.reshape(1, -1)
    ln_g, ln_b = wts["ln_g"], wts["ln_b"]

    def mem_moe(x, l):
        x = _memattn(x, wts["mem_wq"][l], mem_k[l], mem_v[l], wts["mem_wo"][l],
                     row(ln_g[l, 1]), row(ln_b[l, 1]), tiles["mem"])
        return _moe(x, wts["w_router"], row(wts["router_bias"]), wts["moe_wgu"][l], wts["moe_wd"][l],
                    row(ln_g[l, 2]), row(ln_b[l, 2]), tiles["moe"])

    qb, kf, kb, vf, vb, cu, gb = _ab_in(x, wts["w_in_ab"], _rope_tables(pos), tiles["ab_in"])
    attn = moba_fn(qb, kf, kb, vb)
    x = _ab_out(attn, cu, sconv_hist, gb, x, wts["w_out_ab"], wts["sconv_w"],
                row(ln_g[0, 0]), row(ln_b[0, 0]), nseq, tiles["ab_out"])
    sconv_new = cu.reshape(nseq, t, SCONV_CH)[:, t - (SCONV_W - 1):]
    x = mem_moe(x, 0)

    big = _mm(x, wts["w_gdn_main"], tiles["mm"], 1536, "gdn_in")
    qn, kn, v, g, beta = _gdn_prep(big, gdn_hist, wts["gdn_conv_w"], x, wts["w_gdn_a"], wts["w_gdn_b"],
                                   row(wts["gdn_a_log"]), row(wts["gdn_dt_bias"]), nseq, tiles["gdn_prep"])
    c = min(G_CHUNK, t)
    og, gdn_s = _gdn_scan(qn, kn, v, big, g, beta, gdn_s0, row(wts["gdn_norm_w"]), nseq, c)
    gdn_hist_new = big.reshape(nseq, t, -1)[:, t - (G_CONV_W - 1):, :G_QKV_DIM]
    x = _mm_res_ln(og, wts["w_out_gdn"], x, row(ln_g[1, 0]), row(ln_b[1, 0]), tiles["mm"], "gdn_out")
    x = mem_moe(x, 1)
    return x, kf, vf, sconv_new, gdn_s, gdn_hist_new


def kernel(x_prompt, x_sample, mem_prompt, cache_moba_k, cache_moba_v, page_table, state_sconv,
           state_gdn, state_gdn_conv, cache_mem_k, cache_mem_v, w_in_ab, w_out_ab, sconv_w, w_in_gdn,
           gdn_conv_w, gdn_a_log, gdn_dt_bias, gdn_norm_w, w_out_gdn, mem_wq, mem_wk, mem_wv, mem_wo,
           ln_g, ln_b, w_router, router_bias, moe_w_gate, moe_w_up, moe_w_down):
    b, seq, _ = x_prompt.shape
    db, dseq, _ = x_sample.shape
    assert b == 1 and seq % KBLK == 0
    zcols = G_QKV_DIM + G_V_DIM
    wts = dict(
        w_in_ab=w_in_ab.astype(BF16), w_out_ab=w_out_ab.astype(BF16), sconv_w=sconv_w,
        w_gdn_main=w_in_gdn[:, :zcols].astype(BF16),
        w_gdn_a=w_in_gdn[:, zcols:zcols + G_V_HEADS], w_gdn_b=w_in_gdn[:, zcols + G_V_HEADS:],
        gdn_conv_w=gdn_conv_w, gdn_a_log=gdn_a_log, gdn_dt_bias=gdn_dt_bias, gdn_norm_w=gdn_norm_w,
        w_out_gdn=w_out_gdn.astype(BF16),
        mem_wq=mem_wq.astype(BF16), mem_wo=mem_wo.astype(BF16),
        ln_g=ln_g, ln_b=ln_b, w_router=w_router, router_bias=router_bias,
        moe_wgu=jnp.concatenate([moe_w_gate, moe_w_up], axis=-1).astype(BF16),
        moe_wd=moe_w_down.astype(BF16))

    mw = MEM_HEADS * MEM_HD
    w_mem_kv = jnp.concatenate([mem_wk[l] for l in range(DEPTH)] + [mem_wv[l] for l in range(DEPTH)],
                               axis=1).astype(BF16)
    mem_kv = _mm(mem_prompt.reshape(MEM_LEN, D_MODEL), w_mem_kv, MEM_LEN, 1024, "mem_kv")
    mem_k_p = jnp.stack([mem_kv[:, l * mw:(l + 1) * mw] for l in range(DEPTH)]).reshape(DEPTH, b, MEM_LEN, mw)
    mem_v_p = jnp.stack([mem_kv[:, (DEPTH + l) * mw:(DEPTH + l + 1) * mw]
                         for l in range(DEPTH)]).reshape(DEPTH, b, MEM_LEN, mw)

    def moba_p(qb, kf, kb, vb):
        return _moba_prompt(qb, kb, vb, _block_mean(kf))

    tiles_p = dict(ab_in=512, ab_out=512, mem=512, moe=1024, mm=1024, gdn_prep=256)
    y_p, k_p, v_p, sconv_p, gdn_s_p, gdn_conv_p = _trunk(
        x_prompt.reshape(seq, D_MODEL), b, jnp.arange(seq, dtype=jnp.int32), moba_p,
        jnp.zeros((b, SCONV_W - 1, SCONV_CH), F32),
        jnp.zeros((b, G_V_HEADS, G_DK, G_DV), F32),
        jnp.zeros((b, G_CONV_W - 1, G_QKV_DIM), F32),
        mem_k_p, mem_v_p, wts, tiles_p)

    past_len = page_table.shape[1] * PAGE
    pos_s = jnp.tile(past_len + jnp.arange(dseq, dtype=jnp.int32), db)

    def moba_s(qb, kf, kb, vb):
        return _moba_sample(qb, kb, vb, cache_moba_k, cache_moba_v, page_table, dseq)

    ns = db * dseq
    tiles_s = dict(ab_in=ns, ab_out=dseq, mem=dseq, moe=ns, mm=ns, gdn_prep=dseq)
    y_s, k_s, v_s, sconv_s, gdn_s_s, gdn_conv_s = _trunk(
        x_sample.reshape(ns, D_MODEL), db, pos_s, moba_s, state_sconv, state_gdn, state_gdn_conv,
        cache_mem_k.reshape(DEPTH, db, MEM_LEN, mw), cache_mem_v.reshape(DEPTH, db, MEM_LEN, mw),
        wts, tiles_s)

    hd = (HEADS_A, HD_A)
    return (y_p.reshape(b, seq, D_MODEL), y_s.reshape(db, dseq, D_MODEL),
            k_p.reshape(b, seq, *hd), v_p.reshape(b, seq, *hd),
            sconv_p, gdn_s_p, gdn_conv_p,
            mem_k_p.reshape(DEPTH, b, MEM_LEN, MEM_HEADS, MEM_HD),
            mem_v_p.reshape(DEPTH, b, MEM_LEN, MEM_HEADS, MEM_HD),
            k_s.reshape(db, dseq, *hd), v_s.reshape(db, dseq, *hd),
            sconv_s, gdn_s_s, gdn_conv_s)
```

```python
import functools
import math

import jax
import jax.numpy as jnp
from jax import lax
from jax.experimental import pallas as pl
from jax.experimental.pallas import tpu as pltpu

F32 = jnp.float32
BF16 = jnp.bfloat16

D_MODEL = 1024
DEPTH = 2
PAGE = 128
HEADS_A = 8
HD_A = 64
AW = HEADS_A * HD_A
KBLK = 256
TOPK_A = 3
ROPE_THETA = 500000.0
ROPE_DIMS = HD_A // 4
SCONV_CH = 512
SCONV_W = 3
G_QK_HEADS = 8
G_V_HEADS = 16
G_DK = 128
G_DV = 128
G_CONV_W = 4
G_CHUNK = 64
G_QK_DIM = G_QK_HEADS * G_DK
G_V_DIM = G_V_HEADS * G_DV
G_QKV_DIM = 2 * G_QK_DIM + G_V_DIM
MEM_LEN = 256
MEM_HEADS = 4
MEM_HD = D_MODEL // MEM_HEADS
N_EXPERTS = 16
N_GROUPS = 4
EXP_PER_GROUP = N_EXPERTS // N_GROUPS
EXPERT_FF = 512
DN_ALPHA = (2 * DEPTH) ** 0.25
LN_EPS = 1e-5
RMS_EPS = 1e-6
NEG_BIG = -1e30

SUBLANES = 8
MXU_DIM = 256
MIB = 1024 * 1024

_NT = (((1,), (1,)), ((), ()))
_TN = (((0,), (0,)), ((), ()))


def _cp(sem, vmem_mib):
    return pltpu.CompilerParams(dimension_semantics=sem, vmem_limit_bytes=vmem_mib * MIB)


def _dot(a, b):
    return jnp.dot(a, b, preferred_element_type=F32)


def _dot_nt(a, b):
    return lax.dot_general(a, b, _NT, preferred_element_type=F32)


def _split3(a):
    a0 = a.astype(BF16)
    r = a - a0.astype(F32)
    a1 = r.astype(BF16)
    a2 = (r - a1.astype(F32)).astype(BF16)
    return a0, a1, a2


def _dot_bf(a, b):
    return _dot(a.astype(BF16), b.astype(BF16))


def _layernorm(y, g, b):
    mu = jnp.mean(y, axis=-1, keepdims=True)
    yc = y - mu
    var = jnp.mean(yc * yc, axis=-1, keepdims=True)
    return yc * lax.rsqrt(var + LN_EPS) * g + b


def _sigmoid(x):
    return 1.0 / (1.0 + jnp.exp(-x))


def _silu(x):
    return x * _sigmoid(x)


def _mm_kernel(x_ref, w_ref, o_ref, xb_ref):
    @pl.when(pl.program_id(1) == 0)
    def _():
        xb_ref[...] = x_ref[...].astype(BF16)

    o_ref[...] = _dot(xb_ref[...], w_ref[...])


def _mm(x, w, tm, tn, name):
    n, k = x.shape
    f = w.shape[1]
    return pl.pallas_call(
        _mm_kernel,
        grid=(n // tm, f // tn),
        in_specs=[pl.BlockSpec((tm, k), lambda i, j: (i, 0)),
                  pl.BlockSpec((k, tn), lambda i, j: (0, j))],
        out_specs=pl.BlockSpec((tm, tn), lambda i, j: (i, j)),
        out_shape=jax.ShapeDtypeStruct((n, f), F32),
        scratch_shapes=[pltpu.VMEM((tm, k), BF16)],
        compiler_params=_cp(("arbitrary", "arbitrary"), 48),
        name=name,
    )(x, w)


def _mm_res_ln_kernel(a_ref, w_ref, x_ref, g_ref, b_ref, o_ref):
    h = _dot(a_ref[...].astype(BF16), w_ref[...])
    o_ref[...] = _layernorm(DN_ALPHA * x_ref[...] + h, g_ref[...], b_ref[...])


def _mm_res_ln(a, w, x, g, b, tm, name):
    n, k = a.shape
    return pl.pallas_call(
        _mm_res_ln_kernel,
        grid=(n // tm,),
        in_specs=[pl.BlockSpec((tm, k), lambda i: (i, 0)),
                  pl.BlockSpec((k, D_MODEL), lambda i: (0, 0)),
                  pl.BlockSpec((tm, D_MODEL), lambda i: (i, 0)),
                  pl.BlockSpec((1, D_MODEL), lambda i: (0, 0)),
                  pl.BlockSpec((1, D_MODEL), lambda i: (0, 0))],
        out_specs=pl.BlockSpec((tm, D_MODEL), lambda i: (i, 0)),
        out_shape=jax.ShapeDtypeStruct((n, D_MODEL), F32),
        compiler_params=_cp(("arbitrary",), 48),
        name=name,
    )(a, w, x, g, b)


def _ab_in_kernel(x_ref, w_ref, c_ref, sa_ref, sb_ref,
                  qb_ref, kf_ref, kb_ref, vf_ref, vb_ref, cu_ref, gb_ref):
    y = _dot(x_ref[...].astype(BF16), w_ref[...])

    def tile4(t):
        t = t[...]
        return jnp.concatenate([t, t, t, t], axis=1)

    c, sa, sb = tile4(c_ref), tile4(sa_ref), tile4(sb_ref)

    def rope(t):
        return (t * c + pltpu.roll(t, AW - ROPE_DIMS // 2, 1) * sa
                + pltpu.roll(t, ROPE_DIMS // 2, 1) * sb)

    q = rope(y[:, 0:AW])
    k = rope(y[:, AW:2 * AW])
    v = y[:, 2 * AW:3 * AW]
    u = y[:, 3 * AW:3 * AW + SCONV_CH]
    gate_b = y[:, 3 * AW + SCONV_CH:3 * AW + 2 * SCONV_CH]
    gate_c = y[:, 3 * AW + 2 * SCONV_CH:]
    qb_ref[...] = (q * (HD_A ** -0.5)).astype(BF16)
    kf_ref[...] = k
    kb_ref[...] = k.astype(BF16)
    vf_ref[...] = v
    vb_ref[...] = v.astype(BF16)
    cu_ref[...] = gate_c * u
    gb_ref[...] = gate_b


def _ab_in(x, w, tabs, tm):
    n = x.shape[0]
    fw = w.shape[1]
    row = lambda i: (i, 0)
    half = pl.BlockSpec((tm, AW), row)
    tab = pl.BlockSpec((tm, 128), row)
    return pl.pallas_call(
        _ab_in_kernel,
        grid=(n // tm,),
        in_specs=[pl.BlockSpec((tm, D_MODEL), row),
                  pl.BlockSpec((D_MODEL, fw), lambda i: (0, 0)),
                  tab, tab, tab],
        out_specs=[half] * 7,
        out_shape=[jax.ShapeDtypeStruct((n, AW), BF16),
                   jax.ShapeDtypeStruct((n, AW), F32),
                   jax.ShapeDtypeStruct((n, AW), BF16),
                   jax.ShapeDtypeStruct((n, AW), F32),
                   jax.ShapeDtypeStruct((n, AW), BF16),
                   jax.ShapeDtypeStruct((n, AW), F32),
                   jax.ShapeDtypeStruct((n, AW), F32)],
        compiler_params=_cp(("arbitrary",), 48),
        name="ab_in",
    )(x, w, *tabs)


def _rope_tables(pos):
    half = ROPE_DIMS // 2
    inv_freq = ROPE_THETA ** (-2.0 * jnp.arange(half, dtype=F32) / ROPE_DIMS)
    ang = pos.astype(F32)[:, None] * inv_freq[None, :]
    cos, sin = jnp.cos(ang), jnp.sin(ang)
    n = pos.shape[0]
    rest1 = jnp.ones((n, HD_A - ROPE_DIMS), F32)
    rest0 = jnp.zeros((n, HD_A - ROPE_DIMS), F32)
    z = jnp.zeros((n, half), F32)
    c = jnp.concatenate([cos, cos, rest1], axis=1)
    sa = jnp.concatenate([-sin, z, rest0], axis=1)
    sb = jnp.concatenate([z, sin, rest0], axis=1)
    return tuple(jnp.concatenate([t, t], axis=1) for t in (c, sa, sb))


def _block_mean_kernel(k_ref, o_ref):
    o_ref[0] = jnp.sum(k_ref[...], axis=0, keepdims=True) * (1.0 / KBLK)


def _block_mean(k):
    n = k.shape[0]
    nb = n // KBLK
    out = pl.pallas_call(
        _block_mean_kernel,
        grid=(nb,),
        in_specs=[pl.BlockSpec((KBLK, AW), lambda i: (i, 0))],
        out_specs=pl.BlockSpec((1, 1, AW), lambda i: (i, 0, 0)),
        out_shape=jax.ShapeDtypeStruct((nb, 1, AW), F32),
        compiler_params=_cp(("arbitrary",), 32),
        name="moba_block_mean",
    )(k)
    return out.reshape(nb, AW)


def _top3_mask(g, idx, n_idx, axis):
    sel = jnp.zeros(g.shape, jnp.bool_)
    for _ in range(TOPK_A):
        mx = jnp.max(g, axis=axis, keepdims=True)
        first = jnp.min(jnp.where(g == mx, idx, n_idx), axis=axis, keepdims=True)
        pick = idx == first
        sel = jnp.logical_or(sel, pick)
        g = jnp.where(pick, -jnp.inf, g)
    return sel


def _moba_prompt_kernel(q_ref, k_ref, vt2_ref, vto_ref, km_ref, o_ref, bias_ref, s0_ref, s1_ref):
    i = pl.program_id(1)
    nb = km_ref.shape[0]
    q = q_ref[...]
    km = km_ref[...].astype(BF16)
    lane = lax.broadcasted_iota(jnp.int32, q.shape, 1)
    blk = lax.broadcasted_iota(jnp.int32, (nb, KBLK), 0)
    kpos = lax.broadcasted_iota(jnp.int32, (KBLK, KBLK), 0)
    qpos = lax.broadcasted_iota(jnp.int32, (KBLK, KBLK), 1)
    k_own = k_ref[pl.ds(pl.multiple_of(i * KBLK, KBLK), KBLK), :]
    qhs, init = [], []
    for h in range(2):
        qh = jnp.where(lane // HD_A == h, q, jnp.zeros_like(q))
        qhs.append(qh)
        gate = _dot_nt(km, qh)
        gate = jnp.where(blk < i, gate, -jnp.inf)
        sel = jnp.logical_and(_top3_mask(gate, blk, nb, 0), blk < i)
        bias_ref[h, 0:nb, :] = jnp.where(sel, 0.0, NEG_BIG)
        bias_ref[h, nb:nb + SUBLANES, :] = jnp.full((SUBLANES, KBLK), NEG_BIG, F32)
        s = _dot_nt(k_own, qh)
        s = jnp.where(kpos <= qpos, s, NEG_BIG)
        m = jnp.max(s, axis=0, keepdims=True)
        p = jnp.exp(s - m)
        l = jnp.sum(p, axis=0, keepdims=True)
        acc = _dot(vto_ref[0, h * HD_A:(h + 1) * HD_A, :], p.astype(BF16))
        init += [m, l, acc]

    last_pair = nb // 2 - 1

    def qk_into(dst_ref, pair):
        pc = jnp.minimum(pair, last_pair)
        k2 = k_ref[pl.ds(pl.multiple_of(pc * 2 * KBLK, 2 * KBLK), 2 * KBLK), :]
        for h in range(2):
            dst_ref[h] = _dot_nt(k2, qhs[h])

    def consume(src_ref, pair, carry):
        pc = jnp.minimum(pair, last_pair)
        out = []
        for h in range(2):
            m, l, acc = carry[3 * h:3 * h + 3]
            sa = src_ref[h, 0:KBLK, :] + bias_ref[h, pl.ds(2 * pair, 1), :]
            sb = src_ref[h, KBLK:2 * KBLK, :] + bias_ref[h, pl.ds(2 * pair + 1, 1), :]
            m_new = jnp.maximum(m, jnp.maximum(jnp.max(sa, axis=0, keepdims=True),
                                               jnp.max(sb, axis=0, keepdims=True)))
            a = jnp.exp(m - m_new)
            pa = jnp.exp(sa - m_new)
            pb = jnp.exp(sb - m_new)
            l = a * l + jnp.sum(pa, axis=0, keepdims=True) + jnp.sum(pb, axis=0, keepdims=True)
            p2 = jnp.concatenate([pa, pb], axis=0).astype(BF16)
            acc = a * acc + _dot(vt2_ref[pc, h * HD_A:(h + 1) * HD_A, :], p2)
            out += [m_new, l, acc]
        return tuple(out)

    def body(t, carry):
        qk_into(s1_ref, 2 * t + 1)
        carry = consume(s0_ref, 2 * t, carry)
        qk_into(s0_ref, 2 * t + 2)
        return consume(s1_ref, 2 * t + 1, carry)

    npairs = (i + 1) // 2
    qk_into(s0_ref, 0)
    res = lax.fori_loop(0, (npairs + 1) // 2, body, tuple(init))
    outs = [res[2] / res[1], res[5] / res[4]]
    o_ref[...] = jnp.concatenate(outs, axis=0).T


def _moba_prompt(qb, kb, vb, km):
    n = qb.shape[0]
    nb = n // KBLK
    assert nb % 2 == 0
    pairs = AW // 128
    vt_own = vb.reshape(nb, KBLK, AW).transpose(0, 2, 1)
    vt_two = vb.reshape(nb // 2, 2 * KBLK, AW).transpose(0, 2, 1)
    return pl.pallas_call(
        _moba_prompt_kernel,
        grid=(pairs, nb),
        in_specs=[pl.BlockSpec((KBLK, 128), lambda p, i: (i, p)),
                  pl.BlockSpec((n, 128), lambda p, i: (0, p)),
                  pl.BlockSpec((nb // 2, 128, 2 * KBLK), lambda p, i: (0, p, 0)),
                  pl.BlockSpec((1, 128, KBLK), lambda p, i: (i, p, 0)),
                  pl.BlockSpec((nb, 128), lambda p, i: (0, p))],
        out_specs=pl.BlockSpec((KBLK, 128), lambda p, i: (i, p)),
        out_shape=jax.ShapeDtypeStruct((n, AW), F32),
        scratch_shapes=[pltpu.VMEM((2, nb + SUBLANES, KBLK), F32),
                        pltpu.VMEM((2, 2 * KBLK, KBLK), F32),
                        pltpu.VMEM((2, 2 * KBLK, KBLK), F32)],
        compiler_params=_cp(("arbitrary", "arbitrary"), 48),
        name="moba_prompt",
    )(qb, kb, vt_two, vt_own, km)


def _moba_sample_kernel(pt_ref, q_ref, kn_ref, vn_ref, ka_ref, kb_ref, va_ref, vb_ref, o_ref,
                        qexp_ref, qh_ref, km_ref, s_ref, acc_ref, own_ref, l_ref, *, nblk, tq):
    del pt_ref
    j = pl.program_id(1)

    def head_rows(page_ref, h):
        return page_ref[0, pl.ds(h, PAGE, stride=HEADS_A), :]

    def head_block(a_ref, b_ref, h):
        return jnp.concatenate([head_rows(a_ref, h), head_rows(b_ref, h)], axis=0)

    @pl.when(j == 0)
    def _():
        q = q_ref[...]
        lane = lax.broadcasted_iota(jnp.int32, q.shape, 1)
        qexp_ref[...] = jnp.concatenate(
            [jnp.where(lane // HD_A == h, q, jnp.zeros_like(q)) for h in range(HEADS_A)], axis=0)
        qf = q.astype(F32)
        for h in range(HEADS_A):
            qh_ref[h] = qf[:, h * HD_A:(h + 1) * HD_A]

    @pl.when(j < nblk)
    def _():
        parts = []
        for h in range(HEADS_A):
            kh = head_block(ka_ref, kb_ref, h)
            km_ref[h, pl.ds(j, 1), :] = jnp.sum(kh, axis=0, keepdims=True) * (1.0 / KBLK)
            parts.append(_dot_nt(qh_ref[h], kh))
        s_ref[j] = jnp.concatenate(parts, axis=0)

    @pl.when(j == nblk - 1)
    def _():
        qe = qexp_ref[...]
        gate = jnp.concatenate([_dot_nt(qh_ref[h], km_ref[h]) for h in range(HEADS_A)], axis=0)
        bl = lax.broadcasted_iota(jnp.int32, gate.shape, 1)
        self_f = _top3_mask(gate, bl, nblk, 1).astype(F32)

        s_own = _dot_nt(qe, kn_ref[0])
        rq = lax.broadcasted_iota(jnp.int32, s_own.shape, 0) % tq
        kt = lax.broadcasted_iota(jnp.int32, s_own.shape, 1)
        s_own = jnp.where(kt <= rq, s_own, NEG_BIG)
        m = jnp.max(s_own, axis=1, keepdims=True)
        picked = [self_f[:, jj:jj + 1] > 0.0 for jj in range(nblk)]
        for jj in range(nblk):
            sj = jnp.where(picked[jj], s_ref[jj], NEG_BIG)
            m = jnp.maximum(m, jnp.max(sj, axis=1, keepdims=True))
        p_own = jnp.exp(s_own - m)
        l = jnp.sum(p_own, axis=1, keepdims=True)
        for jj in range(nblk):
            pj = jnp.where(picked[jj], jnp.exp(s_ref[jj] - m), 0.0)
            s_ref[jj] = pj
            l = l + jnp.sum(pj, axis=1, keepdims=True)
        l_ref[...] = jnp.broadcast_to(l, l_ref.shape)
        own_ref[...] = _dot(p_own.astype(BF16), vn_ref[0])
        acc_ref[...] = jnp.zeros_like(acc_ref)

    @pl.when(j >= nblk)
    def _():
        pj = s_ref[j - nblk]
        for h in range(HEADS_A):
            rows_h = slice(h * tq, (h + 1) * tq)
            acc_ref[rows_h, :] += _dot(pj[rows_h, :], head_block(va_ref, vb_ref, h))

    @pl.when(j == 2 * nblk - 1)
    def _():
        inv_l = 1.0 / l_ref[:, 0:1]
        own = own_ref[...] * inv_l
        past = acc_ref[...] * inv_l
        lane = lax.broadcasted_iota(jnp.int32, (tq, AW), 1)
        res = jnp.concatenate([past[h * tq:(h + 1) * tq, :] for h in range(HEADS_A)], axis=1)
        for h in range(HEADS_A):
            res = res + jnp.where(lane // HD_A == h, own[h * tq:(h + 1) * tq, :], 0.0)
        o_ref[...] = res


def _moba_sample(qb, kb_new, vb_new, cache_k, cache_v, page_table, tq):
    n = qb.shape[0]
    nseq = n // tq
    npages = page_table.shape[1]
    nblk = npages * PAGE // KBLK
    assert KBLK == 2 * PAGE and tq <= 128
    ck = cache_k.reshape(cache_k.shape[0], PAGE * HEADS_A, HD_A)
    cv = cache_v.reshape(cache_v.shape[0], PAGE * HEADS_A, HD_A)
    pad = ((0, 0), (0, 128 - tq), (0, 0))
    kn = jnp.pad(kb_new.reshape(nseq, tq, AW), pad)
    vn = jnp.pad(vb_new.reshape(nseq, tq, AW), pad)
    rows = HEADS_A * tq

    def kpage(off):
        return pl.BlockSpec((1, PAGE * HEADS_A, HD_A),
                            lambda b, j, pt: (pt[b, 2 * jnp.minimum(j, nblk - 1) + off], 0, 0))

    def vpage(off):
        return pl.BlockSpec((1, PAGE * HEADS_A, HD_A),
                            lambda b, j, pt: (pt[b, 2 * jnp.maximum(j - nblk, 0) + off], 0, 0))

    seq3 = pl.BlockSpec((1, 128, AW), lambda b, j, pt: (b, 0, 0))
    grid_spec = pltpu.PrefetchScalarGridSpec(
        num_scalar_prefetch=1,
        grid=(nseq, 2 * nblk),
        in_specs=[pl.BlockSpec((tq, AW), lambda b, j, pt: (b, 0)), seq3, seq3,
                  kpage(0), kpage(1), vpage(0), vpage(1)],
        out_specs=pl.BlockSpec((tq, AW), lambda b, j, pt: (b, 0)),
        scratch_shapes=[pltpu.VMEM((rows, AW), BF16),
                        pltpu.VMEM((HEADS_A, tq, HD_A), F32),
                        pltpu.VMEM((HEADS_A, nblk, HD_A), F32),
                        pltpu.VMEM((nblk, rows, KBLK), F32),
                        pltpu.VMEM((rows, HD_A), F32),
                        pltpu.VMEM((rows, AW), F32),
                        pltpu.VMEM((rows, 128), F32)])
    return pl.pallas_call(
        functools.partial(_moba_sample_kernel, nblk=nblk, tq=tq),
        grid_spec=grid_spec,
        out_shape=jax.ShapeDtypeStruct((n, AW), F32),
        compiler_params=_cp(("arbitrary", "arbitrary"), 32),
        name="moba_sample",
    )(page_table, qb, kn, vn, ck, ck, cv, cv)


def _ab_out_kernel(attn_ref, cu_ref, cup_ref, hist_ref, gb_ref, x_ref, w_ref, cw_ref, g_ref, b_ref,
                   o_ref, hbuf_ref, *, tt):
    j = pl.program_id(1)
    hbuf_ref[0:SUBLANES, :] = jnp.where(j == 0, hist_ref[0], cup_ref[...])
    hbuf_ref[SUBLANES:, :] = cu_ref[...]
    cw = cw_ref[...]
    conv = jnp.zeros((tt, SCONV_CH), F32)
    for tap in range(SCONV_W):
        r0 = SUBLANES - (SCONV_W - 1) + tap
        conv = conv + hbuf_ref[r0:r0 + tt, :] * cw[tap:tap + 1, :]
    conv = gb_ref[...] * conv
    mix = jnp.concatenate([attn_ref[...], conv], axis=1).astype(BF16)
    h = _dot(mix, w_ref[...])
    o_ref[...] = _layernorm(DN_ALPHA * x_ref[...] + h, g_ref[...], b_ref[...])


def _ab_out(attn, cu, hist, gb, x, w, cw, g, b, nseq, tt):
    n = x.shape[0]
    t = n // nseq
    nt = t // tt
    hist8 = jnp.pad(hist, ((0, 0), (SUBLANES - hist.shape[1], 0), (0, 0)))
    row = lambda s, j: (s * nt + j, 0)
    prev = lambda s, j: (jnp.maximum((s * t + j * tt) // SUBLANES - 1, 0), 0)
    const = lambda s, j: (0, 0)
    return pl.pallas_call(
        functools.partial(_ab_out_kernel, tt=tt),
        grid=(nseq, nt),
        in_specs=[pl.BlockSpec((tt, AW), row),
                  pl.BlockSpec((tt, SCONV_CH), row),
                  pl.BlockSpec((SUBLANES, SCONV_CH), prev),
                  pl.BlockSpec((1, SUBLANES, SCONV_CH), lambda s, j: (s, 0, 0)),
                  pl.BlockSpec((tt, SCONV_CH), row),
                  pl.BlockSpec((tt, D_MODEL), row),
                  pl.BlockSpec((AW + SCONV_CH, D_MODEL), const),
                  pl.BlockSpec((SCONV_W, SCONV_CH), const),
                  pl.BlockSpec((1, D_MODEL), const),
                  pl.BlockSpec((1, D_MODEL), const)],
        out_specs=pl.BlockSpec((tt, D_MODEL), row),
        out_shape=jax.ShapeDtypeStruct((n, D_MODEL), F32),
        scratch_shapes=[pltpu.VMEM((tt + SUBLANES, SCONV_CH), F32)],
        compiler_params=_cp(("arbitrary", "arbitrary"), 48),
        name="ab_out",
    )(attn, cu, cu, hist8, gb, x, w, cw, g, b)


def _memattn_kernel(x_ref, wq_ref, mk_ref, mv_ref, wo_ref, g_ref, b_ref, o_ref):
    x = x_ref[...]
    q = _dot(x.astype(BF16), wq_ref[...])
    mk = mk_ref[0].astype(BF16)
    mv = mv_ref[0].astype(BF16)
    outs = []
    for h in range(MEM_HEADS):
        sl = slice(h * MEM_HD, (h + 1) * MEM_HD)
        s = _dot_nt(q[:, sl].astype(BF16), mk[:, sl]) * (MEM_HD ** -0.5)
        s = s - jnp.max(s, axis=-1, keepdims=True)
        p = jnp.exp(s)
        p = p / jnp.sum(p, axis=-1, keepdims=True)
        outs.append(_dot(p.astype(BF16), mv[:, sl]))
    o = jnp.concatenate(outs, axis=1).astype(BF16)
    y = _dot(o, wo_ref[...])
    o_ref[...] = _layernorm(DN_ALPHA * x + y, g_ref[...], b_ref[...])


def _memattn(x, wq, mk, mv, wo, g, b, tm):
    n = x.shape[0]
    nb = mk.shape[0]
    per = n // tm // nb
    row = lambda i: (i, 0)
    const = lambda i: (0, 0)
    mem = pl.BlockSpec((1, MEM_LEN, D_MODEL), lambda i: (i // per, 0, 0))
    return pl.pallas_call(
        _memattn_kernel,
        grid=(n // tm,),
        in_specs=[pl.BlockSpec((tm, D_MODEL), row),
                  pl.BlockSpec((D_MODEL, D_MODEL), const),
                  mem, mem,
                  pl.BlockSpec((D_MODEL, D_MODEL), const),
                  pl.BlockSpec((1, D_MODEL), const),
                  pl.BlockSpec((1, D_MODEL), const)],
        out_specs=pl.BlockSpec((tm, D_MODEL), row),
        out_shape=jax.ShapeDtypeStruct((n, D_MODEL), F32),
        compiler_params=_cp(("arbitrary",), 48),
        name="memattn",
    )(x, wq, mk, mv, wo, g, b)


def _route(logits, bias):
    lane = lax.broadcasted_iota(jnp.int32, logits.shape, 1)
    grp = lane // EXP_PER_GROUP
    ex = jnp.exp(logits - jnp.max(logits, axis=1, keepdims=True))
    scores = ex / jnp.sum(ex, axis=1, keepdims=True)
    biased = scores + bias

    def top2(vals):
        t1 = jnp.max(vals, axis=1, keepdims=True)
        i1 = jnp.min(jnp.where(vals == t1, lane, N_EXPERTS), axis=1, keepdims=True)
        rest = jnp.where(lane == i1, -jnp.inf, vals)
        t2 = jnp.max(rest, axis=1, keepdims=True)
        i2 = jnp.min(jnp.where(rest == t2, lane, N_EXPERTS), axis=1, keepdims=True)
        return t1, i1, t2, i2

    best = jnp.zeros((logits.shape[0], 1), jnp.int32)
    best_v = None
    for gi in range(N_GROUPS):
        t1, _, t2, _ = top2(jnp.where(grp == gi, biased, -jnp.inf))
        gs = t1 + t2
        if best_v is None:
            best_v = gs
        else:
            better = gs > best_v
            best = jnp.where(better, gi, best)
            best_v = jnp.where(better, gs, best_v)
    _, i1, _, i2 = top2(jnp.where(grp == best, biased, -jnp.inf))
    s1 = jnp.sum(jnp.where(lane == i1, scores, 0.0), axis=1, keepdims=True)
    s2 = jnp.sum(jnp.where(lane == i2, scores, 0.0), axis=1, keepdims=True)
    den = s1 + s2
    return jnp.where(lane == i1, s1 / den, 0.0) + jnp.where(lane == i2, s2 / den, 0.0)


def _moe_kernel(x_ref, wr_ref, rb_ref, wgu_ref, wd_ref, g_ref, b_ref, o_ref,
                xb_ref, comb_ref, acc_ref):
    e = pl.program_id(1)

    @pl.when(e == 0)
    def _():
        x = x_ref[...]
        xb_ref[...] = x.astype(BF16)
        comb_ref[...] = _route(_dot_bf(x, wr_ref[...]), rb_ref[...])
        acc_ref[...] = jnp.zeros_like(acc_ref)

    hgu = _dot(xb_ref[...], wgu_ref[0])
    comb = comb_ref[...]
    lane = lax.broadcasted_iota(jnp.int32, comb.shape, 1)
    ce = jnp.sum(jnp.where(lane == e, comb, 0.0), axis=1, keepdims=True)
    h = _silu(hgu[:, :EXPERT_FF]) * hgu[:, EXPERT_FF:] * ce
    acc_ref[...] += _dot(h.astype(BF16), wd_ref[0])

    @pl.when(e == N_EXPERTS - 1)
    def _():
        o_ref[...] = _layernorm(DN_ALPHA * x_ref[...] + acc_ref[...], g_ref[...], b_ref[...])


def _moe(x, wr, rb, wgu, wd, g, b, tm):
    n = x.shape[0]
    row = lambda i, e: (i, 0)
    const = lambda i, e: (0, 0)
    return pl.pallas_call(
        _moe_kernel,
        grid=(n // tm, N_EXPERTS),
        in_specs=[pl.BlockSpec((tm, D_MODEL), row),
                  pl.BlockSpec((D_MODEL, N_EXPERTS), const),
                  pl.BlockSpec((1, N_EXPERTS), const),
                  pl.BlockSpec((1, D_MODEL, 2 * EXPERT_FF), lambda i, e: (e, 0, 0)),
                  pl.BlockSpec((1, EXPERT_FF, D_MODEL), lambda i, e: (e, 0, 0)),
                  pl.BlockSpec((1, D_MODEL), const),
                  pl.BlockSpec((1, D_MODEL), const)],
        out_specs=pl.BlockSpec((tm, D_MODEL), row),
        out_shape=jax.ShapeDtypeStruct((n, D_MODEL), F32),
        scratch_shapes=[pltpu.VMEM((tm, D_MODEL), BF16),
                        pltpu.VMEM((tm, N_EXPERTS), F32),
                        pltpu.VMEM((tm, D_MODEL), F32)],
        compiler_params=_cp(("arbitrary", "arbitrary"), 48),
        name="moe",
    )(x, wr, rb, wgu, wd, g, b)


def _gdn_prep_kernel(qkv_ref, prev_ref, hist_ref, cw_ref, x_ref, wa_ref, wb_ref, alog_ref, dtb_ref,
                     qn_ref, kn_ref, v_ref, g_ref, beta_ref, hbuf_ref, *, tt):
    j = pl.program_id(1)
    hbuf_ref[0:SUBLANES, :] = jnp.where(j == 0, hist_ref[0], prev_ref[...])
    hbuf_ref[SUBLANES:, :] = qkv_ref[...]
    for c in range(G_QKV_DIM // 128):
        sl = slice(c * 128, (c + 1) * 128)
        y = jnp.zeros((tt, 128), F32)
        for tap in range(G_CONV_W):
            r0 = SUBLANES - (G_CONV_W - 1) + tap
            y = y + hbuf_ref[r0:r0 + tt, sl] * cw_ref[tap:tap + 1, sl]
        y = _silu(y)
        if c < 2 * G_QK_HEADS:
            y = y * lax.rsqrt(jnp.sum(y * y, axis=-1, keepdims=True) + RMS_EPS)
            if c < G_QK_HEADS:
                qn_ref[:, sl] = y * (G_DK ** -0.5)
            else:
                kn_ref[:, (c - G_QK_HEADS) * 128:(c - G_QK_HEADS + 1) * 128] = y
        else:
            v_ref[:, (c - 2 * G_QK_HEADS) * 128:(c - 2 * G_QK_HEADS + 1) * 128] = y
    x = x_ref[...]
    a = _dot_bf(x, wa_ref[...]) + dtb_ref[...]
    softplus = jnp.maximum(a, 0.0) + jnp.log(1.0 + jnp.exp(-jnp.abs(a)))
    g_ref[...] = -jnp.exp(alog_ref[...]) * softplus
    beta_ref[...] = _sigmoid(_dot_bf(x, wb_ref[...]))


def _gdn_prep(big, hist, cw, x, wa, wb, alog, dtb, nseq, tt):
    n = x.shape[0]
    t = n // nseq
    nt = t // tt
    hist8 = jnp.pad(hist, ((0, 0), (SUBLANES - hist.shape[1], 0), (0, 0)))
    row = lambda s, j: (s * nt + j, 0)
    prev = lambda s, j: (jnp.maximum((s * t + j * tt) // SUBLANES - 1, 0), 0)
    const = lambda s, j: (0, 0)
    hv = G_V_HEADS
    return pl.pallas_call(
        functools.partial(_gdn_prep_kernel, tt=tt),
        grid=(nseq, nt),
        in_specs=[pl.BlockSpec((tt, G_QKV_DIM), row),
                  pl.BlockSpec((SUBLANES, G_QKV_DIM), prev),
                  pl.BlockSpec((1, SUBLANES, G_QKV_DIM), lambda s, j: (s, 0, 0)),
                  pl.BlockSpec((G_CONV_W, G_QKV_DIM), const),
                  pl.BlockSpec((tt, D_MODEL), row),
                  pl.BlockSpec((D_MODEL, hv), const),
                  pl.BlockSpec((D_MODEL, hv), const),
                  pl.BlockSpec((1, hv), const),
                  pl.BlockSpec((1, hv), const)],
        out_specs=[pl.BlockSpec((tt, G_QK_DIM), row),
                   pl.BlockSpec((tt, G_QK_DIM), row),
                   pl.BlockSpec((tt, G_V_DIM), row),
                   pl.BlockSpec((tt, hv), row),
                   pl.BlockSpec((tt, hv), row)],
        out_shape=[jax.ShapeDtypeStruct((n, G_QK_DIM), F32),
                   jax.ShapeDtypeStruct((n, G_QK_DIM), F32),
                   jax.ShapeDtypeStruct((n, G_V_DIM), F32),
                   jax.ShapeDtypeStruct((n, hv), F32),
                   jax.ShapeDtypeStruct((n, hv), F32)],
        scratch_shapes=[pltpu.VMEM((tt + SUBLANES, G_QKV_DIM), F32)],
        compiler_params=_cp(("arbitrary", "arbitrary"), 48),
        name="gdn_prep",
    )(big, big, hist8, cw, x, wa, wb, alog, dtb)


def _gdn_geometry(c):
    hpg = min(G_V_HEADS, MXU_DIM // c)
    hpt = 128 // c
    return hpg, G_V_HEADS // hpg, hpt, G_V_HEADS // hpt


def _gdn_local_kernel(q_ref, k_ref, v_ref, g_ref, beta_ref,
                      u_ref, wq_ref, qk_ref, kdt_ref, egl_ref, *, c):
    hpg, ngroups, _, _ = _gdn_geometry(c)
    r = hpg * c
    rep = G_V_HEADS // G_QK_HEADS
    ri = lax.broadcasted_iota(jnp.int32, (c, c), 0)
    ci = lax.broadcasted_iota(jnp.int32, (c, c), 1)
    tri = (ri >= ci).astype(BF16)
    g0, g1, g2 = _split3(g_ref[...])
    gcum = _dot(tri, g0) + _dot(tri, g1) + _dot(tri, g2)
    beta = beta_ref[...]
    egl_ref[0] = jnp.exp(gcum[c - 1:c, :])

    rr = lax.broadcasted_iota(jnp.int32, (r, r), 0)
    cr = lax.broadcasted_iota(jnp.int32, (r, r), 1)
    same = (rr // c) == (cr // c)
    incl = jnp.logical_and(same, rr >= cr)
    strict = jnp.logical_and(same, rr > cr)
    eye = (rr == cr).astype(F32)

    def stack(fn, heads):
        return jnp.concatenate([fn(h) for h in heads], axis=0)

    groups = [list(range(gi * hpg, (gi + 1) * hpg)) for gi in range(ngroups)]
    k_st, q_st, kb_st, gc_col, decay, x, tinv = [], [], [], [], [], [], []
    for heads in groups:
        ks = stack(lambda h: k_ref[:, (h // rep) * G_DK:(h // rep + 1) * G_DK], heads)
        bt = stack(lambda h: beta[:, h:h + 1], heads)
        gc = stack(lambda h: gcum[:, h:h + 1], heads)
        gmat = jnp.broadcast_to(gc, (r, r))
        dec = jnp.exp(jnp.where(incl, gmat - gmat.T, -jnp.inf))
        kb = ks * bt
        xg = -jnp.where(strict, _dot_nt(kb.astype(BF16), ks.astype(BF16)) * dec, 0.0)
        k_st.append(ks); kb_st.append(kb); gc_col.append(gc); decay.append(dec)
        x.append(xg); tinv.append(eye + xg)
    for _ in range(int(math.log2(c)) - 1):
        x = [_dot(xg.astype(BF16), xg.astype(BF16)) for xg in x]
        tinv = [tg + _dot(tg.astype(BF16), xg.astype(BF16)) for tg, xg in zip(tinv, x)]
    for gi, heads in enumerate(groups):
        ks, kb, gc, dec = k_st[gi], kb_st[gi], gc_col[gi], decay[gi]
        tb = tinv[gi].astype(BF16)
        bt = stack(lambda h: beta[:, h:h + 1], heads)
        vs = stack(lambda h: v_ref[:, h * G_DV:(h + 1) * G_DV], heads)
        qs = stack(lambda h: q_ref[:, (h // rep) * G_DK:(h // rep + 1) * G_DK], heads)
        eg = jnp.exp(gc)
        u_st = _dot(tb, (vs * bt).astype(BF16))
        w_st = _dot(tb, (kb * eg).astype(BF16))
        qg_st = qs * eg
        qk_ref[0, gi] = (_dot_nt(qs.astype(BF16), ks.astype(BF16)) * dec).astype(BF16)
        gl = stack(lambda h: jnp.broadcast_to(gcum[c - 1:c, h:h + 1], (c, 1)), heads)
        kdec = ks * jnp.exp(gl - gc)
        for tt in range(r // 128):
            kdt_ref[0, gi * (r // 128) + tt] = kdec[tt * 128:(tt + 1) * 128, :].T.astype(BF16)
        for hl, h in enumerate(heads):
            rows = slice(hl * c, (hl + 1) * c)
            u_ref[:, h * G_DV:(h + 1) * G_DV] = u_st[rows]
            wq_ref[0, h] = jnp.concatenate([w_st[rows], qg_st[rows]], axis=0).astype(BF16)


def _gdn_local(qn, kn, v, g, beta, c):
    n = qn.shape[0]
    nc = n // c
    hv = G_V_HEADS
    hpg, ngroups, _, ntiles = _gdn_geometry(c)
    r = hpg * c
    row = lambda i: (i, 0)
    blk4 = lambda i: (i, 0, 0, 0)
    return pl.pallas_call(
        functools.partial(_gdn_local_kernel, c=c),
        grid=(nc,),
        in_specs=[pl.BlockSpec((c, G_QK_DIM), row),
                  pl.BlockSpec((c, G_QK_DIM), row),
                  pl.BlockSpec((c, G_V_DIM), row),
                  pl.BlockSpec((c, hv), row),
                  pl.BlockSpec((c, hv), row)],
        out_specs=[pl.BlockSpec((c, G_V_DIM), row),
                   pl.BlockSpec((1, hv, 2 * c, G_DK), blk4),
                   pl.BlockSpec((1, ngroups, r, r), blk4),
                   pl.BlockSpec((1, ntiles, 128, 128), blk4),
                   pl.BlockSpec((1, 1, hv), lambda i: (i, 0, 0))],
        out_shape=[jax.ShapeDtypeStruct((n, G_V_DIM), F32),
                   jax.ShapeDtypeStruct((nc, hv, 2 * c, G_DK), BF16),
                   jax.ShapeDtypeStruct((nc, ngroups, r, r), BF16),
                   jax.ShapeDtypeStruct((nc, ntiles, 128, 128), BF16),
                   jax.ShapeDtypeStruct((nc, 1, hv), F32)],
        compiler_params=_cp(("arbitrary",), 48),
        name="gdn_local",
    )(qn, kn, v, g, beta)


def _gdn_scan_kernel(u_ref, wq_ref, qk_ref, kdt_ref, egl_ref, z_ref, s0_ref, nw_ref,
                     o_ref, sfin_ref, s_scr, *, c, nchunks):
    j = pl.program_id(1)

    @pl.when(j == 0)
    def _():
        s_scr[...] = s0_ref[0]

    hpg, ngroups, hpt, ntiles = _gdn_geometry(c)
    egl = egl_ref[0]
    nw = nw_ref[...]
    hv = G_V_HEADS
    s_all = [s_scr[h] for h in range(hv)]
    ws = [_dot(wq_ref[0, h], s_all[h].astype(BF16)) for h in range(hv)]
    vnew = [u_ref[:, h * G_DV:(h + 1) * G_DV] - ws[h][0:c] for h in range(hv)]
    for gi in range(ngroups):
        heads = range(gi * hpg, (gi + 1) * hpg)
        vn_st = jnp.concatenate([vnew[h] for h in heads], axis=0).astype(BF16)
        o_st = jnp.concatenate([ws[h][c:2 * c] for h in heads], axis=0) + _dot(qk_ref[0, gi], vn_st)
        for hl, h in enumerate(heads):
            o = o_st[hl * c:(hl + 1) * c]
            zf = z_ref[:, h * G_DV:(h + 1) * G_DV]
            o = o * lax.rsqrt(jnp.mean(o * o, axis=-1, keepdims=True) + RMS_EPS) * nw * _silu(zf)
            o_ref[:, h * G_DV:(h + 1) * G_DV] = o
    row_head = lax.broadcasted_iota(jnp.int32, (128, G_DV), 0) // c
    for tt in range(ntiles):
        heads = range(tt * hpt, (tt + 1) * hpt)
        vn_t = jnp.concatenate([vnew[h] for h in heads], axis=0)
        vbd = jnp.concatenate([jnp.where(row_head == hl, vn_t, 0.0) for hl in range(hpt)],
                              axis=1).astype(BF16)
        upd = _dot(kdt_ref[0, tt], vbd)
        for hl, h in enumerate(heads):
            s_scr[h] = s_all[h] * egl[:, h:h + 1] + upd[:, hl * G_DV:(hl + 1) * G_DV]

    @pl.when(j == nchunks - 1)
    def _():
        sfin_ref[0] = s_scr[...]


def _gdn_scan(u, wq, qk, kdt, egl, big, s0, nw, nseq, c):
    n = u.shape[0]
    t = n // nseq
    nchunks = t // c
    hv = G_V_HEADS
    hpg, ngroups, _, ntiles = _gdn_geometry(c)
    r = hpg * c
    row = lambda s, j: (s * nchunks + j, 0)
    blk4 = lambda s, j: (s * nchunks + j, 0, 0, 0)
    zcol = G_QKV_DIM // G_V_DIM
    return pl.pallas_call(
        functools.partial(_gdn_scan_kernel, c=c, nchunks=nchunks),
        grid=(nseq, nchunks),
        in_specs=[pl.BlockSpec((c, G_V_DIM), row),
                  pl.BlockSpec((1, hv, 2 * c, G_DK), blk4),
                  pl.BlockSpec((1, ngroups, r, r), blk4),
                  pl.BlockSpec((1, ntiles, 128, 128), blk4),
                  pl.BlockSpec((1, 1, hv), lambda s, j: (s * nchunks + j, 0, 0)),
                  pl.BlockSpec((c, G_V_DIM), lambda s, j: (s * nchunks + j, zcol)),
                  pl.BlockSpec((1, hv, G_DK, G_DV), lambda s, j: (s, 0, 0, 0)),
                  pl.BlockSpec((1, G_DV), lambda s, j: (0, 0))],
        out_specs=[pl.BlockSpec((c, G_V_DIM), row),
                   pl.BlockSpec((1, hv, G_DK, G_DV), lambda s, j: (s, 0, 0, 0))],
        out_shape=[jax.ShapeDtypeStruct((n, G_V_DIM), F32),
                   jax.ShapeDtypeStruct((nseq, hv, G_DK, G_DV), F32)],
        scratch_shapes=[pltpu.VMEM((hv, G_DK, G_DV), F32)],
        compiler_params=_cp(("arbitrary", "arbitrary"), 48),
        name="gdn_scan",
    )(u, wq, qk, kdt, egl, big, s0, nw)


def _trunk(x, nseq, pos, moba_fn, sconv_hist, gdn_s0, gdn_hist, mem_k, mem_v, wts, tiles):
    n = x.shape[0]
    t = n // nseq
    assert t >= G_CONV_W - 1 and t >= SCONV_W - 1
    row = lambda a: a.reshape(1, -1)
    ln_g, ln_b = wts["ln_g"], wts["ln_b"]

    def mem_moe(x, l):
        x = _memattn(x, wts["mem_wq"][l], mem_k[l], mem_v[l], wts["mem_wo"][l],
                     row(ln_g[l, 1]), row(ln_b[l, 1]), tiles["mem"])
        return _moe(x, wts["w_router"], row(wts["router_bias"]), wts["moe_wgu"][l], wts["moe_wd"][l],
                    row(ln_g[l, 2]), row(ln_b[l, 2]), tiles["moe"])

    qb, kf, kb, vf, vb, cu, gb = _ab_in(x, wts["w_in_ab"], _rope_tables(pos), tiles["ab_in"])
    attn = moba_fn(qb, kf, kb, vb)
    x = _ab_out(attn, cu, sconv_hist, gb, x, wts["w_out_ab"], wts["sconv_w"],
                row(ln_g[0, 0]), row(ln_b[0, 0]), nseq, tiles["ab_out"])
    sconv_new = cu.reshape(nseq, t, SCONV_CH)[:, t - (SCONV_W - 1):]
    x = mem_moe(x, 0)

    big = _mm(x, wts["w_gdn_main"], tiles["mm"], 1536, "gdn_in")
    qn, kn, v, g, beta = _gdn_prep(big, gdn_hist, wts["gdn_conv_w"], x, wts["w_gdn_a"], wts["w_gdn_b"],
                                   row(wts["gdn_a_log"]), row(wts["gdn_dt_bias"]), nseq, tiles["gdn_prep"])
    c = min(G_CHUNK, t)
    u, wq, qk, kdt, egl = _gdn_local(qn, kn, v, g, beta, c)
    og, gdn_s = _gdn_scan(u, wq, qk, kdt, egl, big, gdn_s0, row(wts["gdn_norm_w"]), nseq, c)
    gdn_hist_new = big.reshape(nseq, t, -1)[:, t - (G_CONV_W - 1):, :G_QKV_DIM]
    x = _mm_res_ln(og, wts["w_out_gdn"], x, row(ln_g[1, 0]), row(ln_b[1, 0]), tiles["mm"], "gdn_out")
    x = mem_moe(x, 1)
    return x, kf, vf, sconv_new, gdn_s, gdn_hist_new


def kernel(x_prompt, x_sample, mem_prompt, cache_moba_k, cache_moba_v, page_table, state_sconv,
           state_gdn, state_gdn_conv, cache_mem_k, cache_mem_v, w_in_ab, w_out_ab, sconv_w, w_in_gdn,
           gdn_conv_w, gdn_a_log, gdn_dt_bias, gdn_norm_w, w_out_gdn, mem_wq, mem_wk, mem_wv, mem_wo,
           ln_g, ln_b, w_router, router_bias, moe_w_gate, moe_w_up, moe_w_down):
    b, seq, _ = x_prompt.shape
    db, dseq, _ = x_sample.shape
    assert b == 1 and seq % KBLK == 0
    zcols = G_QKV_DIM + G_V_DIM
    wts = dict(
        w_in_ab=w_in_ab.astype(BF16), w_out_ab=w_out_ab.astype(BF16), sconv_w=sconv_w,
        w_gdn_main=w_in_gdn[:, :zcols].astype(BF16),
        w_gdn_a=w_in_gdn[:, zcols:zcols + G_V_HEADS], w_gdn_b=w_in_gdn[:, zcols + G_V_HEADS:],
        gdn_conv_w=gdn_conv_w, gdn_a_log=gdn_a_log, gdn_dt_bias=gdn_dt_bias, gdn_norm_w=gdn_norm_w,
        w_out_gdn=w_out_gdn.astype(BF16),
        mem_wq=mem_wq.astype(BF16), mem_wo=mem_wo.astype(BF16),
        ln_g=ln_g, ln_b=ln_b, w_router=w_router, router_bias=router_bias,
        moe_wgu=jnp.concatenate([moe_w_gate, moe_w_up], axis=-1).astype(BF16),
        moe_wd=moe_w_down.astype(BF16))

    mw = MEM_HEADS * MEM_HD
    w_mem_kv = jnp.concatenate([mem_wk[l] for l in range(DEPTH)] + [mem_wv[l] for l in range(DEPTH)],
                               axis=1).astype(BF16)
    mem_kv = _mm(mem_prompt.reshape(MEM_LEN, D_MODEL), w_mem_kv, MEM_LEN, 1024, "mem_kv")
    mem_k_p = jnp.stack([mem_kv[:, l * mw:(l + 1) * mw] for l in range(DEPTH)]).reshape(DEPTH, b, MEM_LEN, mw)
    mem_v_p = jnp.stack([mem_kv[:, (DEPTH + l) * mw:(DEPTH + l + 1) * mw]
                         for l in range(DEPTH)]).reshape(DEPTH, b, MEM_LEN, mw)

    def moba_p(qb, kf, kb, vb):
        return _moba_prompt(qb, kb, vb, _block_mean(kf))

    tiles_p = dict(ab_in=512, ab_out=512, mem=512, moe=1024, mm=1024, gdn_prep=256)
    y_p, k_p, v_p, sconv_p, gdn_s_p, gdn_conv_p = _trunk(
        x_prompt.reshape(seq, D_MODEL), b, jnp.arange(seq, dtype=jnp.int32), moba_p,
        jnp.zeros((b, SCONV_W - 1, SCONV_CH), F32),
        jnp.zeros((b, G_V_HEADS, G_DK, G_DV), F32),
        jnp.zeros((b, G_CONV_W - 1, G_QKV_DIM), F32),
        mem_k_p, mem_v_p, wts, tiles_p)

    past_len = page_table.shape[1] * PAGE
    pos_s = jnp.tile(past_len + jnp.arange(dseq, dtype=jnp.int32), db)

    def moba_s(qb, kf, kb, vb):
        return _moba_sample(qb, kb, vb, cache_moba_k, cache_moba_v, page_table, dseq)

    ns = db * dseq
    tiles_s = dict(ab_in=ns, ab_out=dseq, mem=dseq, moe=ns, mm=ns, gdn_prep=dseq)
    y_s, k_s, v_s, sconv_s, gdn_s_s, gdn_conv_s = _trunk(
        x_sample.reshape(ns, D_MODEL), db, pos_s, moba_s, state_sconv, state_gdn, state_gdn_conv,
        cache_mem_k.reshape(DEPTH, db, MEM_LEN, mw), cache_mem_v.reshape(DEPTH, db, MEM_LEN, mw),
        wts, tiles_s)

    hd = (HEADS_A, HD_A)
    return (y_p.reshape(b, seq, D_MODEL), y_s.reshape(db, dseq, D_MODEL),
            k_p.reshape(b, seq, *hd), v_p.reshape(b, seq, *hd),
            sconv_p, gdn_s_p, gdn_conv_p,
            mem_k_p.reshape(DEPTH, b, MEM_LEN, MEM_HEADS, MEM_HD),
            mem_v_p.reshape(DEPTH, b, MEM_LEN, MEM_HEADS, MEM_HD),
            k_s.reshape(db, dseq, *hd), v_s.reshape(db, dseq, *hd),
            sconv_s, gdn_s_s, gdn_conv_s)
```

```python
import functools
import math

import jax
import jax.numpy as jnp
from jax import lax
from jax.experimental import pallas as pl
from jax.experimental.pallas import tpu as pltpu

F32 = jnp.float32
BF16 = jnp.bfloat16

D_MODEL = 1024
DEPTH = 2
PAGE = 128
HEADS_A = 8
HD_A = 64
AW = HEADS_A * HD_A
KBLK = 256
TOPK_A = 3
ROPE_THETA = 500000.0
ROPE_DIMS = HD_A // 4
SCONV_CH = 512
SCONV_W = 3
G_QK_HEADS = 8
G_V_HEADS = 16
G_DK = 128
G_DV = 128
G_CONV_W = 4
G_CHUNK = 64
G_QK_DIM = G_QK_HEADS * G_DK
G_V_DIM = G_V_HEADS * G_DV
G_QKV_DIM = 2 * G_QK_DIM + G_V_DIM
MEM_LEN = 256
MEM_HEADS = 4
MEM_HD = D_MODEL // MEM_HEADS
N_EXPERTS = 16
N_GROUPS = 4
EXP_PER_GROUP = N_EXPERTS // N_GROUPS
EXPERT_FF = 512
DN_ALPHA = (2 * DEPTH) ** 0.25
LN_EPS = 1e-5
RMS_EPS = 1e-6
NEG_BIG = -1e30

SUBLANES = 8
MXU_DIM = 256
SAMPLE_PAGES_PER_STEP = 4
MIB = 1024 * 1024

_NT = (((1,), (1,)), ((), ()))
_TN = (((0,), (0,)), ((), ()))


def _cp(sem, vmem_mib):
    return pltpu.CompilerParams(dimension_semantics=sem, vmem_limit_bytes=vmem_mib * MIB)


def _dot(a, b):
    return jnp.dot(a, b, preferred_element_type=F32)


def _dot_nt(a, b):
    return lax.dot_general(a, b, _NT, preferred_element_type=F32)


def _split3(a):
    a0 = a.astype(BF16)
    r = a - a0.astype(F32)
    a1 = r.astype(BF16)
    a2 = (r - a1.astype(F32)).astype(BF16)
    return a0, a1, a2


def _dot_bf(a, b):
    return _dot(a.astype(BF16), b.astype(BF16))


def _layernorm(y, g, b):
    mu = jnp.mean(y, axis=-1, keepdims=True)
    yc = y - mu
    var = jnp.mean(yc * yc, axis=-1, keepdims=True)
    return yc * lax.rsqrt(var + LN_EPS) * g + b


def _sigmoid(x):
    return 1.0 / (1.0 + jnp.exp(-x))


def _silu(x):
    return x * _sigmoid(x)


def _mm_kernel(x_ref, w_ref, o_ref, xb_ref):
    @pl.when(pl.program_id(1) == 0)
    def _():
        xb_ref[...] = x_ref[...].astype(BF16)

    o_ref[...] = _dot(xb_ref[...], w_ref[...])


def _mm(x, w, tm, tn, name):
    n, k = x.shape
    f = w.shape[1]
    return pl.pallas_call(
        _mm_kernel,
        grid=(n // tm, f // tn),
        in_specs=[pl.BlockSpec((tm, k), lambda i, j: (i, 0)),
                  pl.BlockSpec((k, tn), lambda i, j: (0, j))],
        out_specs=pl.BlockSpec((tm, tn), lambda i, j: (i, j)),
        out_shape=jax.ShapeDtypeStruct((n, f), F32),
        scratch_shapes=[pltpu.VMEM((tm, k), BF16)],
        compiler_params=_cp(("arbitrary", "arbitrary"), 48),
        name=name,
    )(x, w)


def _mm_res_ln_kernel(a_ref, w_ref, x_ref, g_ref, b_ref, o_ref):
    h = _dot(a_ref[...].astype(BF16), w_ref[...])
    o_ref[...] = _layernorm(DN_ALPHA * x_ref[...] + h, g_ref[...], b_ref[...])


def _mm_res_ln(a, w, x, g, b, tm, name):
    n, k = a.shape
    return pl.pallas_call(
        _mm_res_ln_kernel,
        grid=(n // tm,),
        in_specs=[pl.BlockSpec((tm, k), lambda i: (i, 0)),
                  pl.BlockSpec((k, D_MODEL), lambda i: (0, 0)),
                  pl.BlockSpec((tm, D_MODEL), lambda i: (i, 0)),
                  pl.BlockSpec((1, D_MODEL), lambda i: (0, 0)),
                  pl.BlockSpec((1, D_MODEL), lambda i: (0, 0))],
        out_specs=pl.BlockSpec((tm, D_MODEL), lambda i: (i, 0)),
        out_shape=jax.ShapeDtypeStruct((n, D_MODEL), F32),
        compiler_params=_cp(("arbitrary",), 48),
        name=name,
    )(a, w, x, g, b)


def _ab_in_kernel(x_ref, w_ref, c_ref, sa_ref, sb_ref,
                  qb_ref, kf_ref, kb_ref, vf_ref, vb_ref, cu_ref, gb_ref):
    y = _dot(x_ref[...].astype(BF16), w_ref[...])

    def tile4(t):
        t = t[...]
        return jnp.concatenate([t, t, t, t], axis=1)

    c, sa, sb = tile4(c_ref), tile4(sa_ref), tile4(sb_ref)

    def rope(t):
        return (t * c + pltpu.roll(t, AW - ROPE_DIMS // 2, 1) * sa
                + pltpu.roll(t, ROPE_DIMS // 2, 1) * sb)

    q = rope(y[:, 0:AW])
    k = rope(y[:, AW:2 * AW])
    v = y[:, 2 * AW:3 * AW]
    u = y[:, 3 * AW:3 * AW + SCONV_CH]
    gate_b = y[:, 3 * AW + SCONV_CH:3 * AW + 2 * SCONV_CH]
    gate_c = y[:, 3 * AW + 2 * SCONV_CH:]
    qb_ref[...] = (q * (HD_A ** -0.5)).astype(BF16)
    kf_ref[...] = k
    kb_ref[...] = k.astype(BF16)
    vf_ref[...] = v
    vb_ref[...] = v.astype(BF16)
    cu_ref[...] = gate_c * u
    gb_ref[...] = gate_b


def _ab_in(x, w, tabs, tm):
    n = x.shape[0]
    fw = w.shape[1]
    row = lambda i: (i, 0)
    half = pl.BlockSpec((tm, AW), row)
    tab = pl.BlockSpec((tm, 128), row)
    return pl.pallas_call(
        _ab_in_kernel,
        grid=(n // tm,),
        in_specs=[pl.BlockSpec((tm, D_MODEL), row),
                  pl.BlockSpec((D_MODEL, fw), lambda i: (0, 0)),
                  tab, tab, tab],
        out_specs=[half] * 7,
        out_shape=[jax.ShapeDtypeStruct((n, AW), BF16),
                   jax.ShapeDtypeStruct((n, AW), F32),
                   jax.ShapeDtypeStruct((n, AW), BF16),
                   jax.ShapeDtypeStruct((n, AW), F32),
                   jax.ShapeDtypeStruct((n, AW), BF16),
                   jax.ShapeDtypeStruct((n, AW), F32),
                   jax.ShapeDtypeStruct((n, AW), F32)],
        compiler_params=_cp(("arbitrary",), 48),
        name="ab_in",
    )(x, w, *tabs)


def _rope_tables(pos):
    half = ROPE_DIMS // 2
    inv_freq = ROPE_THETA ** (-2.0 * jnp.arange(half, dtype=F32) / ROPE_DIMS)
    ang = pos.astype(F32)[:, None] * inv_freq[None, :]
    cos, sin = jnp.cos(ang), jnp.sin(ang)
    n = pos.shape[0]
    rest1 = jnp.ones((n, HD_A - ROPE_DIMS), F32)
    rest0 = jnp.zeros((n, HD_A - ROPE_DIMS), F32)
    z = jnp.zeros((n, half), F32)
    c = jnp.concatenate([cos, cos, rest1], axis=1)
    sa = jnp.concatenate([-sin, z, rest0], axis=1)
    sb = jnp.concatenate([z, sin, rest0], axis=1)
    return tuple(jnp.concatenate([t, t], axis=1) for t in (c, sa, sb))


def _block_mean_kernel(k_ref, o_ref):
    o_ref[0] = jnp.sum(k_ref[...], axis=0, keepdims=True) * (1.0 / KBLK)


def _block_mean(k):
    n = k.shape[0]
    nb = n // KBLK
    out = pl.pallas_call(
        _block_mean_kernel,
        grid=(nb,),
        in_specs=[pl.BlockSpec((KBLK, AW), lambda i: (i, 0))],
        out_specs=pl.BlockSpec((1, 1, AW), lambda i: (i, 0, 0)),
        out_shape=jax.ShapeDtypeStruct((nb, 1, AW), F32),
        compiler_params=_cp(("arbitrary",), 32),
        name="moba_block_mean",
    )(k)
    return out.reshape(nb, AW)


def _top3_mask(g, idx, n_idx, axis):
    sel = jnp.zeros(g.shape, jnp.bool_)
    for _ in range(TOPK_A):
        mx = jnp.max(g, axis=axis, keepdims=True)
        first = jnp.min(jnp.where(g == mx, idx, n_idx), axis=axis, keepdims=True)
        pick = idx == first
        sel = jnp.logical_or(sel, pick)
        g = jnp.where(pick, -jnp.inf, g)
    return sel


def _moba_prompt_kernel(q_ref, k_ref, vt2_ref, vto_ref, km_ref, o_ref, bias_ref, s0_ref, s1_ref):
    i = pl.program_id(1)
    nb = km_ref.shape[0]
    q = q_ref[...]
    km = km_ref[...].astype(BF16)
    lane = lax.broadcasted_iota(jnp.int32, q.shape, 1)
    blk = lax.broadcasted_iota(jnp.int32, (nb, KBLK), 0)
    kpos = lax.broadcasted_iota(jnp.int32, (KBLK, KBLK), 0)
    qpos = lax.broadcasted_iota(jnp.int32, (KBLK, KBLK), 1)
    k_own = k_ref[pl.ds(pl.multiple_of(i * KBLK, KBLK), KBLK), :]
    qhs, init = [], []
    for h in range(2):
        qh = jnp.where(lane // HD_A == h, q, jnp.zeros_like(q))
        qhs.append(qh)
        gate = _dot_nt(km, qh)
        gate = jnp.where(blk < i, gate, -jnp.inf)
        sel = jnp.logical_and(_top3_mask(gate, blk, nb, 0), blk < i)
        bias_ref[h, 0:nb, :] = jnp.where(sel, 0.0, NEG_BIG)
        bias_ref[h, nb:nb + SUBLANES, :] = jnp.full((SUBLANES, KBLK), NEG_BIG, F32)
        s = _dot_nt(k_own, qh)
        s = jnp.where(kpos <= qpos, s, NEG_BIG)
        m = jnp.max(s, axis=0, keepdims=True)
        p = jnp.exp(s - m)
        l = jnp.sum(p, axis=0, keepdims=True)
        acc = _dot(vto_ref[0, h * HD_A:(h + 1) * HD_A, :], p.astype(BF16))
        init += [m, l, acc]

    last_pair = nb // 2 - 1

    def qk_into(dst_ref, pair):
        pc = jnp.minimum(pair, last_pair)
        k2 = k_ref[pl.ds(pl.multiple_of(pc * 2 * KBLK, 2 * KBLK), 2 * KBLK), :]
        for h in range(2):
            dst_ref[h] = _dot_nt(k2, qhs[h])

    def consume(src_ref, pair, carry):
        pc = jnp.minimum(pair, last_pair)
        out = []
        for h in range(2):
            m, l, acc = carry[3 * h:3 * h + 3]
            sa = src_ref[h, 0:KBLK, :] + bias_ref[h, pl.ds(2 * pair, 1), :]
            sb = src_ref[h, KBLK:2 * KBLK, :] + bias_ref[h, pl.ds(2 * pair + 1, 1), :]
            m_new = jnp.maximum(m, jnp.maximum(jnp.max(sa, axis=0, keepdims=True),
                                               jnp.max(sb, axis=0, keepdims=True)))
            a = jnp.exp(m - m_new)
            pa = jnp.exp(sa - m_new)
            pb = jnp.exp(sb - m_new)
            l = a * l + jnp.sum(pa, axis=0, keepdims=True) + jnp.sum(pb, axis=0, keepdims=True)
            p2 = jnp.concatenate([pa, pb], axis=0).astype(BF16)
            acc = a * acc + _dot(vt2_ref[pc, h * HD_A:(h + 1) * HD_A, :], p2)
            out += [m_new, l, acc]
        return tuple(out)

    def body(t, carry):
        qk_into(s1_ref, 2 * t + 1)
        carry = consume(s0_ref, 2 * t, carry)
        qk_into(s0_ref, 2 * t + 2)
        return consume(s1_ref, 2 * t + 1, carry)

    npairs = (i + 1) // 2
    qk_into(s0_ref, 0)
    res = lax.fori_loop(0, (npairs + 1) // 2, body, tuple(init))
    outs = [res[2] / res[1], res[5] / res[4]]
    o_ref[...] = jnp.concatenate(outs, axis=0).T


def _moba_prompt(qb, kb, vb, km):
    n = qb.shape[0]
    nb = n // KBLK
    assert nb % 2 == 0
    pairs = AW // 128
    vt_own = vb.reshape(nb, KBLK, AW).transpose(0, 2, 1)
    vt_two = vb.reshape(nb // 2, 2 * KBLK, AW).transpose(0, 2, 1)
    return pl.pallas_call(
        _moba_prompt_kernel,
        grid=(pairs, nb),
        in_specs=[pl.BlockSpec((KBLK, 128), lambda p, i: (i, p)),
                  pl.BlockSpec((n, 128), lambda p, i: (0, p)),
                  pl.BlockSpec((nb // 2, 128, 2 * KBLK), lambda p, i: (0, p, 0)),
                  pl.BlockSpec((1, 128, KBLK), lambda p, i: (i, p, 0)),
                  pl.BlockSpec((nb, 128), lambda p, i: (0, p))],
        out_specs=pl.BlockSpec((KBLK, 128), lambda p, i: (i, p)),
        out_shape=jax.ShapeDtypeStruct((n, AW), F32),
        scratch_shapes=[pltpu.VMEM((2, nb + SUBLANES, KBLK), F32),
                        pltpu.VMEM((2, 2 * KBLK, KBLK), F32),
                        pltpu.VMEM((2, 2 * KBLK, KBLK), F32)],
        compiler_params=_cp(("arbitrary", "arbitrary"), 48),
        name="moba_prompt",
    )(qb, kb, vt_two, vt_own, km)


def _moba_sample_kernel(pt_ref, q_ref, kn_ref, vn_ref, *rest, nblk, tq):
    del pt_ref
    npg = SAMPLE_PAGES_PER_STEP
    k_refs, v_refs = rest[:npg], rest[npg:2 * npg]
    o_ref, qexp_ref, gate_ref, s_ref, acc_ref, l_ref = rest[2 * npg:]
    j = pl.program_id(1)
    nsteps = 2 * nblk // npg
    bps = npg // 2

    @pl.when(j == 0)
    def _():
        q = q_ref[...]
        lane = lax.broadcasted_iota(jnp.int32, q.shape, 1)
        qexp_ref[...] = jnp.concatenate(
            [jnp.where(lane // HD_A == h, q, jnp.zeros_like(q)) for h in range(HEADS_A)], axis=0)
        gate_ref[...] = jnp.full(gate_ref.shape, -jnp.inf, F32)

    @pl.when(j < nsteps)
    def _():
        qe = qexp_ref[...]
        bl = lax.broadcasted_iota(jnp.int32, gate_ref.shape, 1)
        for bi in range(bps):
            pa = k_refs[2 * bi][0]
            pb = k_refs[2 * bi + 1][0]
            blk = j * bps + bi
            s_ref[blk] = jnp.concatenate([_dot(qe, pa.astype(BF16)), _dot(qe, pb.astype(BF16))], axis=1)
            km = (jnp.sum(pa, axis=1, keepdims=True) + jnp.sum(pb, axis=1, keepdims=True)) * (1.0 / KBLK)
            g = _dot(qe, jnp.broadcast_to(km, (AW, 128)).astype(BF16))
            gate_ref[...] = jnp.where(bl == blk, g, gate_ref[...])

    @pl.when(j == nsteps - 1)
    def _():
        qe = qexp_ref[...]
        gate = gate_ref[...]
        bl = lax.broadcasted_iota(jnp.int32, gate.shape, 1)
        self_f = jnp.logical_and(_top3_mask(gate, bl, gate.shape[1], 1), bl < nblk).astype(F32)

        s_own = _dot_nt(qe, kn_ref[0])
        rq = lax.broadcasted_iota(jnp.int32, s_own.shape, 0) % tq
        kt = lax.broadcasted_iota(jnp.int32, s_own.shape, 1)
        s_own = jnp.where(kt <= rq, s_own, NEG_BIG)
        m = jnp.max(s_own, axis=1, keepdims=True)
        picked = [self_f[:, jj:jj + 1] > 0.0 for jj in range(nblk)]
        for jj in range(nblk):
            sj = jnp.where(picked[jj], s_ref[jj], NEG_BIG)
            m = jnp.maximum(m, jnp.max(sj, axis=1, keepdims=True))
        p_own = jnp.exp(s_own - m)
        l = jnp.sum(p_own, axis=1, keepdims=True)
        for jj in range(nblk):
            pj = jnp.where(picked[jj], jnp.exp(s_ref[jj] - m), 0.0)
            s_ref[jj] = pj
            l = l + jnp.sum(pj, axis=1, keepdims=True)
        l_ref[...] = jnp.broadcast_to(l, l_ref.shape)
        acc_ref[...] = _dot(p_own.astype(BF16), vn_ref[0])

    @pl.when(j >= nsteps)
    def _():
        acc = acc_ref[...]
        for bi in range(bps):
            p = s_ref[(j - nsteps) * bps + bi].astype(BF16)
            acc = acc + _dot_nt(p[:, :PAGE], v_refs[2 * bi][0].astype(BF16))
            acc = acc + _dot_nt(p[:, PAGE:], v_refs[2 * bi + 1][0].astype(BF16))
        acc_ref[...] = acc

    @pl.when(j == 2 * nsteps - 1)
    def _():
        o = acc_ref[...] / l_ref[:, 0:1]
        lane = lax.broadcasted_iota(jnp.int32, (tq, AW), 1)
        res = jnp.zeros((tq, AW), F32)
        for h in range(HEADS_A):
            res = res + jnp.where(lane // HD_A == h, o[h * tq:(h + 1) * tq, :], 0.0)
        o_ref[...] = res


def _moba_sample(qb, kb_new, vb_new, cache_k, cache_v, page_table, tq):
    n = qb.shape[0]
    nseq = n // tq
    npages = page_table.shape[1]
    nblk = npages * PAGE // KBLK
    npg = SAMPLE_PAGES_PER_STEP
    assert KBLK == 2 * PAGE and tq <= 128 and nblk <= 128 and npages % npg == 0
    nsteps = npages // npg
    ck = cache_k.transpose(0, 2, 3, 1).reshape(cache_k.shape[0], AW, PAGE)
    cv = cache_v.transpose(0, 2, 3, 1).reshape(cache_v.shape[0], AW, PAGE)
    pad = ((0, 0), (0, 128 - tq), (0, 0))
    kn = jnp.pad(kb_new.reshape(nseq, tq, AW), pad)
    vn = jnp.pad(vb_new.reshape(nseq, tq, AW), pad)
    rows = HEADS_A * tq

    def kpage(off):
        return pl.BlockSpec((1, AW, PAGE),
                            lambda b, j, pt: (pt[b, npg * jnp.minimum(j, nsteps - 1) + off], 0, 0))

    def vpage(off):
        return pl.BlockSpec((1, AW, PAGE),
                            lambda b, j, pt: (pt[b, npg * jnp.maximum(j - nsteps, 0) + off], 0, 0))

    seq3 = pl.BlockSpec((1, 128, AW), lambda b, j, pt: (b, 0, 0))
    grid_spec = pltpu.PrefetchScalarGridSpec(
        num_scalar_prefetch=1,
        grid=(nseq, 2 * nsteps),
        in_specs=([pl.BlockSpec((tq, AW), lambda b, j, pt: (b, 0)), seq3, seq3]
                  + [kpage(off) for off in range(npg)] + [vpage(off) for off in range(npg)]),
        out_specs=pl.BlockSpec((tq, AW), lambda b, j, pt: (b, 0)),
        scratch_shapes=[pltpu.VMEM((rows, AW), BF16),
                        pltpu.VMEM((rows, 128), F32),
                        pltpu.VMEM((nblk, rows, KBLK), F32),
                        pltpu.VMEM((rows, AW), F32),
                        pltpu.VMEM((rows, 128), F32)])
    return pl.pallas_call(
        functools.partial(_moba_sample_kernel, nblk=nblk, tq=tq),
        grid_spec=grid_spec,
        out_shape=jax.ShapeDtypeStruct((n, AW), F32),
        compiler_params=_cp(("arbitrary", "arbitrary"), 32),
        name="moba_sample",
    )(page_table, qb, kn, vn, *([ck] * npg), *([cv] * npg))


def _ab_out_kernel(attn_ref, cu_ref, cup_ref, hist_ref, gb_ref, x_ref, w_ref, cw_ref, g_ref, b_ref,
                   o_ref, hbuf_ref, *, tt):
    j = pl.program_id(1)
    hbuf_ref[0:SUBLANES, :] = jnp.where(j == 0, hist_ref[0], cup_ref[...])
    hbuf_ref[SUBLANES:, :] = cu_ref[...]
    cw = cw_ref[...]
    conv = jnp.zeros((tt, SCONV_CH), F32)
    for tap in range(SCONV_W):
        r0 = SUBLANES - (SCONV_W - 1) + tap
        conv = conv + hbuf_ref[r0:r0 + tt, :] * cw[tap:tap + 1, :]
    conv = gb_ref[...] * conv
    mix = jnp.concatenate([attn_ref[...], conv], axis=1).astype(BF16)
    h = _dot(mix, w_ref[...])
    o_ref[...] = _layernorm(DN_ALPHA * x_ref[...] + h, g_ref[...], b_ref[...])


def _ab_out(attn, cu, hist, gb, x, w, cw, g, b, nseq, tt):
    n = x.shape[0]
    t = n // nseq
    nt = t // tt
    hist8 = jnp.pad(hist, ((0, 0), (SUBLANES - hist.shape[1], 0), (0, 0)))
    row = lambda s, j: (s * nt + j, 0)
    prev = lambda s, j: (jnp.maximum((s * t + j * tt) // SUBLANES - 1, 0), 0)
    const = lambda s, j: (0, 0)
    return pl.pallas_call(
        functools.partial(_ab_out_kernel, tt=tt),
        grid=(nseq, nt),
        in_specs=[pl.BlockSpec((tt, AW), row),
                  pl.BlockSpec((tt, SCONV_CH), row),
                  pl.BlockSpec((SUBLANES, SCONV_CH), prev),
                  pl.BlockSpec((1, SUBLANES, SCONV_CH), lambda s, j: (s, 0, 0)),
                  pl.BlockSpec((tt, SCONV_CH), row),
                  pl.BlockSpec((tt, D_MODEL), row),
                  pl.BlockSpec((AW + SCONV_CH, D_MODEL), const),
                  pl.BlockSpec((SCONV_W, SCONV_CH), const),
                  pl.BlockSpec((1, D_MODEL), const),
                  pl.BlockSpec((1, D_MODEL), const)],
        out_specs=pl.BlockSpec((tt, D_MODEL), row),
        out_shape=jax.ShapeDtypeStruct((n, D_MODEL), F32),
        scratch_shapes=[pltpu.VMEM((tt + SUBLANES, SCONV_CH), F32)],
        compiler_params=_cp(("arbitrary", "arbitrary"), 48),
        name="ab_out",
    )(attn, cu, cu, hist8, gb, x, w, cw, g, b)


def _memattn_kernel(x_ref, wq_ref, mk_ref, mv_ref, wo_ref, g_ref, b_ref, o_ref):
    x = x_ref[...]
    q = _dot(x.astype(BF16), wq_ref[...])
    mk = mk_ref[0].astype(BF16)
    mv = mv_ref[0].astype(BF16)
    outs = []
    for h in range(MEM_HEADS):
        sl = slice(h * MEM_HD, (h + 1) * MEM_HD)
        s = _dot_nt(q[:, sl].astype(BF16), mk[:, sl]) * (MEM_HD ** -0.5)
        s = s - jnp.max(s, axis=-1, keepdims=True)
        p = jnp.exp(s)
        p = p / jnp.sum(p, axis=-1, keepdims=True)
        outs.append(_dot(p.astype(BF16), mv[:, sl]))
    o = jnp.concatenate(outs, axis=1).astype(BF16)
    y = _dot(o, wo_ref[...])
    o_ref[...] = _layernorm(DN_ALPHA * x + y, g_ref[...], b_ref[...])


def _memattn(x, wq, mk, mv, wo, g, b, tm):
    n = x.shape[0]
    nb = mk.shape[0]
    per = n // tm // nb
    row = lambda i: (i, 0)
    const = lambda i: (0, 0)
    mem = pl.BlockSpec((1, MEM_LEN, D_MODEL), lambda i: (i // per, 0, 0))
    return pl.pallas_call(
        _memattn_kernel,
        grid=(n // tm,),
        in_specs=[pl.BlockSpec((tm, D_MODEL), row),
                  pl.BlockSpec((D_MODEL, D_MODEL), const),
                  mem, mem,
                  pl.BlockSpec((D_MODEL, D_MODEL), const),
                  pl.BlockSpec((1, D_MODEL), const),
                  pl.BlockSpec((1, D_MODEL), const)],
        out_specs=pl.BlockSpec((tm, D_MODEL), row),
        out_shape=jax.ShapeDtypeStruct((n, D_MODEL), F32),
        compiler_params=_cp(("arbitrary",), 48),
        name="memattn",
    )(x, wq, mk, mv, wo, g, b)


def _route(logits, bias):
    lane = lax.broadcasted_iota(jnp.int32, logits.shape, 1)
    grp = lane // EXP_PER_GROUP
    ex = jnp.exp(logits - jnp.max(logits, axis=1, keepdims=True))
    scores = ex / jnp.sum(ex, axis=1, keepdims=True)
    biased = scores + bias

    def top2(vals):
        t1 = jnp.max(vals, axis=1, keepdims=True)
        i1 = jnp.min(jnp.where(vals == t1, lane, N_EXPERTS), axis=1, keepdims=True)
        rest = jnp.where(lane == i1, -jnp.inf, vals)
        t2 = jnp.max(rest, axis=1, keepdims=True)
        i2 = jnp.min(jnp.where(rest == t2, lane, N_EXPERTS), axis=1, keepdims=True)
        return t1, i1, t2, i2

    best = jnp.zeros((logits.shape[0], 1), jnp.int32)
    best_v = None
    for gi in range(N_GROUPS):
        t1, _, t2, _ = top2(jnp.where(grp == gi, biased, -jnp.inf))
        gs = t1 + t2
        if best_v is None:
            best_v = gs
        else:
            better = gs > best_v
            best = jnp.where(better, gi, best)
            best_v = jnp.where(better, gs, best_v)
    _, i1, _, i2 = top2(jnp.where(grp == best, biased, -jnp.inf))
    s1 = jnp.sum(jnp.where(lane == i1, scores, 0.0), axis=1, keepdims=True)
    s2 = jnp.sum(jnp.where(lane == i2, scores, 0.0), axis=1, keepdims=True)
    den = s1 + s2
    return jnp.where(lane == i1, s1 / den, 0.0) + jnp.where(lane == i2, s2 / den, 0.0)


def _moe_kernel(x_ref, wr_ref, rb_ref, wgu_ref, wd_ref, g_ref, b_ref, o_ref,
                xb_ref, comb_ref, acc_ref):
    e = pl.program_id(1)

    @pl.when(e == 0)
    def _():
        x = x_ref[...]
        xb_ref[...] = x.astype(BF16)
        comb_ref[...] = _route(_dot_bf(x, wr_ref[...]), rb_ref[...])
        acc_ref[...] = jnp.zeros_like(acc_ref)

    hgu = _dot(xb_ref[...], wgu_ref[0])
    comb = comb_ref[...]
    lane = lax.broadcasted_iota(jnp.int32, comb.shape, 1)
    ce = jnp.sum(jnp.where(lane == e, comb, 0.0), axis=1, keepdims=True)
    h = _silu(hgu[:, :EXPERT_FF]) * hgu[:, EXPERT_FF:] * ce
    acc_ref[...] += _dot(h.astype(BF16), wd_ref[0])

    @pl.when(e == N_EXPERTS - 1)
    def _():
        o_ref[...] = _layernorm(DN_ALPHA * x_ref[...] + acc_ref[...], g_ref[...], b_ref[...])


def _moe(x, wr, rb, wgu, wd, g, b, tm):
    n = x.shape[0]
    row = lambda i, e: (i, 0)
    const = lambda i, e: (0, 0)
    return pl.pallas_call(
        _moe_kernel,
        grid=(n // tm, N_EXPERTS),
        in_specs=[pl.BlockSpec((tm, D_MODEL), row),
                  pl.BlockSpec((D_MODEL, N_EXPERTS), const),
                  pl.BlockSpec((1, N_EXPERTS), const),
                  pl.BlockSpec((1, D_MODEL, 2 * EXPERT_FF), lambda i, e: (e, 0, 0)),
                  pl.BlockSpec((1, EXPERT_FF, D_MODEL), lambda i, e: (e, 0, 0)),
                  pl.BlockSpec((1, D_MODEL), const),
                  pl.BlockSpec((1, D_MODEL), const)],
        out_specs=pl.BlockSpec((tm, D_MODEL), row),
        out_shape=jax.ShapeDtypeStruct((n, D_MODEL), F32),
        scratch_shapes=[pltpu.VMEM((tm, D_MODEL), BF16),
                        pltpu.VMEM((tm, N_EXPERTS), F32),
                        pltpu.VMEM((tm, D_MODEL), F32)],
        compiler_params=_cp(("arbitrary", "arbitrary"), 48),
        name="moe",
    )(x, wr, rb, wgu, wd, g, b)


def _gdn_prep_kernel(qkv_ref, prev_ref, hist_ref, cw_ref, x_ref, wa_ref, wb_ref, alog_ref, dtb_ref,
                     qn_ref, kn_ref, v_ref, g_ref, beta_ref, hbuf_ref, *, tt):
    j = pl.program_id(1)
    hbuf_ref[0:SUBLANES, :] = jnp.where(j == 0, hist_ref[0], prev_ref[...])
    hbuf_ref[SUBLANES:, :] = qkv_ref[...]
    for c in range(G_QKV_DIM // 128):
        sl = slice(c * 128, (c + 1) * 128)
        y = jnp.zeros((tt, 128), F32)
        for tap in range(G_CONV_W):
            r0 = SUBLANES - (G_CONV_W - 1) + tap
            y = y + hbuf_ref[r0:r0 + tt, sl] * cw_ref[tap:tap + 1, sl]
        y = _silu(y)
        if c < 2 * G_QK_HEADS:
            y = y * lax.rsqrt(jnp.sum(y * y, axis=-1, keepdims=True) + RMS_EPS)
            if c < G_QK_HEADS:
                qn_ref[:, sl] = y * (G_DK ** -0.5)
            else:
                kn_ref[:, (c - G_QK_HEADS) * 128:(c - G_QK_HEADS + 1) * 128] = y
        else:
            v_ref[:, (c - 2 * G_QK_HEADS) * 128:(c - 2 * G_QK_HEADS + 1) * 128] = y
    x = x_ref[...]
    a = _dot_bf(x, wa_ref[...]) + dtb_ref[...]
    softplus = jnp.maximum(a, 0.0) + jnp.log(1.0 + jnp.exp(-jnp.abs(a)))
    g_ref[...] = -jnp.exp(alog_ref[...]) * softplus
    beta_ref[...] = _sigmoid(_dot_bf(x, wb_ref[...]))


def _gdn_prep(big, hist, cw, x, wa, wb, alog, dtb, nseq, tt):
    n = x.shape[0]
    t = n // nseq
    nt = t // tt
    hist8 = jnp.pad(hist, ((0, 0), (SUBLANES - hist.shape[1], 0), (0, 0)))
    row = lambda s, j: (s * nt + j, 0)
    prev = lambda s, j: (jnp.maximum((s * t + j * tt) // SUBLANES - 1, 0), 0)
    const = lambda s, j: (0, 0)
    hv = G_V_HEADS
    return pl.pallas_call(
        functools.partial(_gdn_prep_kernel, tt=tt),
        grid=(nseq, nt),
        in_specs=[pl.BlockSpec((tt, G_QKV_DIM), row),
                  pl.BlockSpec((SUBLANES, G_QKV_DIM), prev),
                  pl.BlockSpec((1, SUBLANES, G_QKV_DIM), lambda s, j: (s, 0, 0)),
                  pl.BlockSpec((G_CONV_W, G_QKV_DIM), const),
                  pl.BlockSpec((tt, D_MODEL), row),
                  pl.BlockSpec((D_MODEL, hv), const),
                  pl.BlockSpec((D_MODEL, hv), const),
                  pl.BlockSpec((1, hv), const),
                  pl.BlockSpec((1, hv), const)],
        out_specs=[pl.BlockSpec((tt, G_QK_DIM), row),
                   pl.BlockSpec((tt, G_QK_DIM), row),
                   pl.BlockSpec((tt, G_V_DIM), row),
                   pl.BlockSpec((tt, hv), row),
                   pl.BlockSpec((tt, hv), row)],
        out_shape=[jax.ShapeDtypeStruct((n, G_QK_DIM), F32),
                   jax.ShapeDtypeStruct((n, G_QK_DIM), F32),
                   jax.ShapeDtypeStruct((n, G_V_DIM), F32),
                   jax.ShapeDtypeStruct((n, hv), F32),
                   jax.ShapeDtypeStruct((n, hv), F32)],
        scratch_shapes=[pltpu.VMEM((tt + SUBLANES, G_QKV_DIM), F32)],
        compiler_params=_cp(("arbitrary", "arbitrary"), 48),
        name="gdn_prep",
    )(big, big, hist8, cw, x, wa, wb, alog, dtb)


def _gdn_geometry(c):
    hpg = min(G_V_HEADS, MXU_DIM // c)
    hpt = 128 // c
    return hpg, G_V_HEADS // hpg, hpt, G_V_HEADS // hpt


def _gdn_local_kernel(q_ref, k_ref, v_ref, g_ref, beta_ref,
                      u_ref, wq_ref, qk_ref, kdt_ref, egl_ref, *, c):
    hpg, ngroups, _, _ = _gdn_geometry(c)
    r = hpg * c
    rep = G_V_HEADS // G_QK_HEADS
    ri = lax.broadcasted_iota(jnp.int32, (c, c), 0)
    ci = lax.broadcasted_iota(jnp.int32, (c, c), 1)
    tri = (ri >= ci).astype(BF16)
    g0, g1, g2 = _split3(g_ref[...])
    gcum = _dot(tri, g0) + _dot(tri, g1) + _dot(tri, g2)
    beta = beta_ref[...]
    egl_ref[0] = jnp.exp(gcum[c - 1:c, :])

    rr = lax.broadcasted_iota(jnp.int32, (r, r), 0)
    cr = lax.broadcasted_iota(jnp.int32, (r, r), 1)
    same = (rr // c) == (cr // c)
    incl = jnp.logical_and(same, rr >= cr)
    strict = jnp.logical_and(same, rr > cr)
    eye = (rr == cr).astype(F32)

    def stack(fn, heads):
        return jnp.concatenate([fn(h) for h in heads], axis=0)

    groups = [list(range(gi * hpg, (gi + 1) * hpg)) for gi in range(ngroups)]
    k_st, q_st, kb_st, gc_col, decay, x, tinv = [], [], [], [], [], [], []
    for heads in groups:
        ks = stack(lambda h: k_ref[:, (h // rep) * G_DK:(h // rep + 1) * G_DK], heads)
        bt = stack(lambda h: beta[:, h:h + 1], heads)
        gc = stack(lambda h: gcum[:, h:h + 1], heads)
        gmat = jnp.broadcast_to(gc, (r, r))
        dec = jnp.exp(jnp.where(incl, gmat - gmat.T, -jnp.inf))
        kb = ks * bt
        xg = -jnp.where(strict, _dot_nt(kb.astype(BF16), ks.astype(BF16)) * dec, 0.0)
        k_st.append(ks); kb_st.append(kb); gc_col.append(gc); decay.append(dec)
        x.append(xg); tinv.append(eye + xg)
    for _ in range(int(math.log2(c)) - 1):
        x = [_dot(xg.astype(BF16), xg.astype(BF16)) for xg in x]
        tinv = [tg + _dot(tg.astype(BF16), xg.astype(BF16)) for tg, xg in zip(tinv, x)]
    for gi, heads in enumerate(groups):
        ks, kb, gc, dec = k_st[gi], kb_st[gi], gc_col[gi], decay[gi]
        tb = tinv[gi].astype(BF16)
        bt = stack(lambda h: beta[:, h:h + 1], heads)
        vs = stack(lambda h: v_ref[:, h * G_DV:(h + 1) * G_DV], heads)
        qs = stack(lambda h: q_ref[:, (h // rep) * G_DK:(h // rep + 1) * G_DK], heads)
        eg = jnp.exp(gc)
        u_st = _dot(tb, (vs * bt).astype(BF16))
        w_st = _dot(tb, (kb * eg).astype(BF16))
        qg_st = qs * eg
        qk_ref[0, gi] = (_dot_nt(qs.astype(BF16), ks.astype(BF16)) * dec).astype(BF16)
        gl = stack(lambda h: jnp.broadcast_to(gcum[c - 1:c, h:h + 1], (c, 1)), heads)
        kdec = ks * jnp.exp(gl - gc)
        for tt in range(r // 128):
            kdt_ref[0, gi * (r // 128) + tt] = kdec[tt * 128:(tt + 1) * 128, :].T.astype(BF16)
        for hl, h in enumerate(heads):
            rows = slice(hl * c, (hl + 1) * c)
            u_ref[:, h * G_DV:(h + 1) * G_DV] = u_st[rows]
            wq_ref[0, h] = jnp.concatenate([w_st[rows], qg_st[rows]], axis=0).astype(BF16)


def _gdn_local(qn, kn, v, g, beta, c):
    n = qn.shape[0]
    nc = n // c
    hv = G_V_HEADS
    hpg, ngroups, _, ntiles = _gdn_geometry(c)
    r = hpg * c
    row = lambda i: (i, 0)
    blk4 = lambda i: (i, 0, 0, 0)
    return pl.pallas_call(
        functools.partial(_gdn_local_kernel, c=c),
        grid=(nc,),
        in_specs=[pl.BlockSpec((c, G_QK_DIM), row),
                  pl.BlockSpec((c, G_QK_DIM), row),
                  pl.BlockSpec((c, G_V_DIM), row),
                  pl.BlockSpec((c, hv), row),
                  pl.BlockSpec((c, hv), row)],
        out_specs=[pl.BlockSpec((c, G_V_DIM), row),
                   pl.BlockSpec((1, hv, 2 * c, G_DK), blk4),
                   pl.BlockSpec((1, ngroups, r, r), blk4),
                   pl.BlockSpec((1, ntiles, 128, 128), blk4),
                   pl.BlockSpec((1, 1, hv), lambda i: (i, 0, 0))],
        out_shape=[jax.ShapeDtypeStruct((n, G_V_DIM), F32),
                   jax.ShapeDtypeStruct((nc, hv, 2 * c, G_DK), BF16),
                   jax.ShapeDtypeStruct((nc, ngroups, r, r), BF16),
                   jax.ShapeDtypeStruct((nc, ntiles, 128, 128), BF16),
                   jax.ShapeDtypeStruct((nc, 1, hv), F32)],
        compiler_params=_cp(("arbitrary",), 48),
        name="gdn_local",
    )(qn, kn, v, g, beta)


def _gdn_scan_kernel(u_ref, wq_ref, qk_ref, kdt_ref, egl_ref, z_ref, s0_ref, nw_ref,
                     o_ref, sfin_ref, s_scr, *, c, nchunks):
    j = pl.program_id(1)

    @pl.when(j == 0)
    def _():
        s_scr[...] = s0_ref[0]

    hpg, ngroups, hpt, ntiles = _gdn_geometry(c)
    egl = egl_ref[0]
    nw = nw_ref[...]
    hv = G_V_HEADS
    s_all = [s_scr[h] for h in range(hv)]
    ws = [_dot(wq_ref[0, h], s_all[h].astype(BF16)) for h in range(hv)]
    vnew = [u_ref[:, h * G_DV:(h + 1) * G_DV] - ws[h][0:c] for h in range(hv)]
    for gi in range(ngroups):
        heads = range(gi * hpg, (gi + 1) * hpg)
        vn_st = jnp.concatenate([vnew[h] for h in heads], axis=0).astype(BF16)
        o_st = jnp.concatenate([ws[h][c:2 * c] for h in heads], axis=0) + _dot(qk_ref[0, gi], vn_st)
        for hl, h in enumerate(heads):
            o = o_st[hl * c:(hl + 1) * c]
            zf = z_ref[:, h * G_DV:(h + 1) * G_DV]
            o = o * lax.rsqrt(jnp.mean(o * o, axis=-1, keepdims=True) + RMS_EPS) * nw * _silu(zf)
            o_ref[:, h * G_DV:(h + 1) * G_DV] = o
    row_head = lax.broadcasted_iota(jnp.int32, (128, G_DV), 0) // c
    for tt in range(ntiles):
        heads = range(tt * hpt, (tt + 1) * hpt)
        vn_t = jnp.concatenate([vnew[h] for h in heads], axis=0)
        vbd = jnp.concatenate([jnp.where(row_head == hl, vn_t, 0.0) for hl in range(hpt)],
                              axis=1).astype(BF16)
        upd = _dot(kdt_ref[0, tt], vbd)
        for hl, h in enumerate(heads):
            s_scr[h] = s_all[h] * egl[:, h:h + 1] + upd[:, hl * G_DV:(hl + 1) * G_DV]

    @pl.when(j == nchunks - 1)
    def _():
        sfin_ref[0] = s_scr[...]


def _gdn_scan(u, wq, qk, kdt, egl, big, s0, nw, nseq, c):
    n = u.shape[0]
    t = n // nseq
    nchunks = t // c
    hv = G_V_HEADS
    hpg, ngroups, _, ntiles = _gdn_geometry(c)
    r = hpg * c
    row = lambda s, j: (s * nchunks + j, 0)
    blk4 = lambda s, j: (s * nchunks + j, 0, 0, 0)
    zcol = G_QKV_DIM // G_V_DIM
    return pl.pallas_call(
        functools.partial(_gdn_scan_kernel, c=c, nchunks=nchunks),
        grid=(nseq, nchunks),
        in_specs=[pl.BlockSpec((c, G_V_DIM), row),
                  pl.BlockSpec((1, hv, 2 * c, G_DK), blk4),
                  pl.BlockSpec((1, ngroups, r, r), blk4),
                  pl.BlockSpec((1, ntiles, 128, 128), blk4),
                  pl.BlockSpec((1, 1, hv), lambda s, j: (s * nchunks + j, 0, 0)),
                  pl.BlockSpec((c, G_V_DIM), lambda s, j: (s * nchunks + j, zcol)),
                  pl.BlockSpec((1, hv, G_DK, G_DV), lambda s, j: (s, 0, 0, 0)),
                  pl.BlockSpec((1, G_DV), lambda s, j: (0, 0))],
        out_specs=[pl.BlockSpec((c, G_V_DIM), row),
                   pl.BlockSpec((1, hv, G_DK, G_DV), lambda s, j: (s, 0, 0, 0))],
        out_shape=[jax.ShapeDtypeStruct((n, G_V_DIM), F32),
                   jax.ShapeDtypeStruct((nseq, hv, G_DK, G_DV), F32)],
        scratch_shapes=[pltpu.VMEM((hv, G_DK, G_DV), F32)],
        compiler_params=_cp(("arbitrary", "arbitrary"), 48),
        name="gdn_scan",
    )(u, wq, qk, kdt, egl, big, s0, nw)


def _trunk(x, nseq, pos, moba_fn, sconv_hist, gdn_s0, gdn_hist, mem_k, mem_v, wts, tiles):
    n = x.shape[0]
    t = n // nseq
    assert t >= G_CONV_W - 1 and t >= SCONV_W - 1
    row = lambda a: a.reshape(1, -1)
    ln_g, ln_b = wts["ln_g"], wts["ln_b"]

    def mem_moe(x, l):
        x = _memattn(x, wts["mem_wq"][l], mem_k[l], mem_v[l], wts["mem_wo"][l],
                     row(ln_g[l, 1]), row(ln_b[l, 1]), tiles["mem"])
        return _moe(x, wts["w_router"], row(wts["router_bias"]), wts["moe_wgu"][l], wts["moe_wd"][l],
                    row(ln_g[l, 2]), row(ln_b[l, 2]), tiles["moe"])

    qb, kf, kb, vf, vb, cu, gb = _ab_in(x, wts["w_in_ab"], _rope_tables(pos), tiles["ab_in"])
    attn = moba_fn(qb, kf, kb, vb)
    x = _ab_out(attn, cu, sconv_hist, gb, x, wts["w_out_ab"], wts["sconv_w"],
                row(ln_g[0, 0]), row(ln_b[0, 0]), nseq, tiles["ab_out"])
    sconv_new = cu.reshape(nseq, t, SCONV_CH)[:, t - (SCONV_W - 1):]
    x = mem_moe(x, 0)

    big = _mm(x, wts["w_gdn_main"], tiles["mm"], 1536, "gdn_in")
    qn, kn, v, g, beta = _gdn_prep(big, gdn_hist, wts["gdn_conv_w"], x, wts["w_gdn_a"], wts["w_gdn_b"],
                                   row(wts["gdn_a_log"]), row(wts["gdn_dt_bias"]), nseq, tiles["gdn_prep"])
    c = min(G_CHUNK, t)
    u, wq, qk, kdt, egl = _gdn_local(qn, kn, v, g, beta, c)
    og, gdn_s = _gdn_scan(u, wq, qk, kdt, egl, big, gdn_s0, row(wts["gdn_norm_w"]), nseq, c)
    gdn_hist_new = big.reshape(nseq, t, -1)[:, t - (G_CONV_W - 1):, :G_QKV_DIM]
    x = _mm_res_ln(og, wts["w_out_gdn"], x, row(ln_g[1, 0]), row(ln_b[1, 0]), tiles["mm"], "gdn_out")
    x = mem_moe(x, 1)
    return x, kf, vf, sconv_new, gdn_s, gdn_hist_new


def kernel(x_prompt, x_sample, mem_prompt, cache_moba_k, cache_moba_v, page_table, state_sconv,
           state_gdn, state_gdn_conv, cache_mem_k, cache_mem_v, w_in_ab, w_out_ab, sconv_w, w_in_gdn,
           gdn_conv_w, gdn_a_log, gdn_dt_bias, gdn_norm_w, w_out_gdn, mem_wq, mem_wk, mem_wv, mem_wo,
           ln_g, ln_b, w_router, router_bias, moe_w_gate, moe_w_up, moe_w_down):
    b, seq, _ = x_prompt.shape
    db, dseq, _ = x_sample.shape
    assert b == 1 and seq % KBLK == 0
    zcols = G_QKV_DIM + G_V_DIM
    wts = dict(
        w_in_ab=w_in_ab.astype(BF16), w_out_ab=w_out_ab.astype(BF16), sconv_w=sconv_w,
        w_gdn_main=w_in_gdn[:, :zcols].astype(BF16),
        w_gdn_a=w_in_gdn[:, zcols:zcols + G_V_HEADS], w_gdn_b=w_in_gdn[:, zcols + G_V_HEADS:],
        gdn_conv_w=gdn_conv_w, gdn_a_log=gdn_a_log, gdn_dt_bias=gdn_dt_bias, gdn_norm_w=gdn_norm_w,
        w_out_gdn=w_out_gdn.astype(BF16),
        mem_wq=mem_wq.astype(BF16), mem_wo=mem_wo.astype(BF16),
        ln_g=ln_g, ln_b=ln_b, w_router=w_router, router_bias=router_bias,
        moe_wgu=jnp.concatenate([moe_w_gate, moe_w_up], axis=-1).astype(BF16),
        moe_wd=moe_w_down.astype(BF16))

    mw = MEM_HEADS * MEM_HD
    w_mem_kv = jnp.concatenate([mem_wk[l] for l in range(DEPTH)] + [mem_wv[l] for l in range(DEPTH)],
                               axis=1).astype(BF16)
    mem_kv = _mm(mem_prompt.reshape(MEM_LEN, D_MODEL), w_mem_kv, MEM_LEN, 1024, "mem_kv")
    mem_k_p = jnp.stack([mem_kv[:, l * mw:(l + 1) * mw] for l in range(DEPTH)]).reshape(DEPTH, b, MEM_LEN, mw)
    mem_v_p = jnp.stack([mem_kv[:, (DEPTH + l) * mw:(DEPTH + l + 1) * mw]
                         for l in range(DEPTH)]).reshape(DEPTH, b, MEM_LEN, mw)

    def moba_p(qb, kf, kb, vb):
        return _moba_prompt(qb, kb, vb, _block_mean(kf))

    tiles_p = dict(ab_in=512, ab_out=512, mem=512, moe=1024, mm=1024, gdn_prep=256)
    y_p, k_p, v_p, sconv_p, gdn_s_p, gdn_conv_p = _trunk(
        x_prompt.reshape(seq, D_MODEL), b, jnp.arange(seq, dtype=jnp.int32), moba_p,
        jnp.zeros((b, SCONV_W - 1, SCONV_CH), F32),
        jnp.zeros((b, G_V_HEADS, G_DK, G_DV), F32),
        jnp.zeros((b, G_CONV_W - 1, G_QKV_DIM), F32),
        mem_k_p, mem_v_p, wts, tiles_p)

    past_len = page_table.shape[1] * PAGE
    pos_s = jnp.tile(past_len + jnp.arange(dseq, dtype=jnp.int32), db)

    def moba_s(qb, kf, kb, vb):
        return _moba_sample(qb, kb, vb, cache_moba_k, cache_moba_v, page_table, dseq)

    ns = db * dseq
    tiles_s = dict(ab_in=ns, ab_out=dseq, mem=dseq, moe=ns, mm=ns, gdn_prep=dseq)
    y_s, k_s, v_s, sconv_s, gdn_s_s, gdn_conv_s = _trunk(
        x_sample.reshape(ns, D_MODEL), db, pos_s, moba_s, state_sconv, state_gdn, state_gdn_conv,
        cache_mem_k.reshape(DEPTH, db, MEM_LEN, mw), cache_mem_v.reshape(DEPTH, db, MEM_LEN, mw),
        wts, tiles_s)

    hd = (HEADS_A, HD_A)
    return (y_p.reshape(b, seq, D_MODEL), y_s.reshape(db, dseq, D_MODEL),
            k_p.reshape(b, seq, *hd), v_p.reshape(b, seq, *hd),
            sconv_p, gdn_s_p, gdn_conv_p,
            mem_k_p.reshape(DEPTH, b, MEM_LEN, MEM_HEADS, MEM_HD),
            mem_v_p.reshape(DEPTH, b, MEM_LEN, MEM_HEADS, MEM_HD),
            k_s.reshape(db, dseq, *hd), v_s.reshape(db, dseq, *hd),
            sconv_s, gdn_s_s, gdn_conv_s)
```

```python
import functools
import math

import jax
import jax.numpy as jnp
from jax import lax
from jax.experimental import pallas as pl
from jax.experimental.pallas import tpu as pltpu

F32 = jnp.float32
BF16 = jnp.bfloat16

D_MODEL = 1024
DEPTH = 2
PAGE = 128
HEADS_A = 8
HD_A = 64
AW = HEADS_A * HD_A
KBLK = 256
TOPK_A = 3
ROPE_THETA = 500000.0
ROPE_DIMS = HD_A // 4
SCONV_CH = 512
SCONV_W = 3
G_QK_HEADS = 8
G_V_HEADS = 16
G_DK = 128
G_DV = 128
G_CONV_W = 4
G_CHUNK = 64
G_QK_DIM = G_QK_HEADS * G_DK
G_V_DIM = G_V_HEADS * G_DV
G_QKV_DIM = 2 * G_QK_DIM + G_V_DIM
MEM_LEN = 256
MEM_HEADS = 4
MEM_HD = D_MODEL // MEM_HEADS
N_EXPERTS = 16
N_GROUPS = 4
EXP_PER_GROUP = N_EXPERTS // N_GROUPS
EXPERT_FF = 512
DN_ALPHA = (2 * DEPTH) ** 0.25
LN_EPS = 1e-5
RMS_EPS = 1e-6
NEG_BIG = -1e30

SUBLANES = 8
MXU_DIM = 256
SAMPLE_PAGES_PER_STEP = 8
MIB = 1024 * 1024

_NT = (((1,), (1,)), ((), ()))
_TN = (((0,), (0,)), ((), ()))


def _cp(sem, vmem_mib):
    return pltpu.CompilerParams(dimension_semantics=sem, vmem_limit_bytes=vmem_mib * MIB)


def _dot(a, b):
    return jnp.dot(a, b, preferred_element_type=F32)


def _dot_nt(a, b):
    return lax.dot_general(a, b, _NT, preferred_element_type=F32)


def _split3(a):
    a0 = a.astype(BF16)
    r = a - a0.astype(F32)
    a1 = r.astype(BF16)
    a2 = (r - a1.astype(F32)).astype(BF16)
    return a0, a1, a2


def _dot_bf(a, b):
    return _dot(a.astype(BF16), b.astype(BF16))


def _layernorm(y, g, b):
    mu = jnp.mean(y, axis=-1, keepdims=True)
    yc = y - mu
    var = jnp.mean(yc * yc, axis=-1, keepdims=True)
    return yc * lax.rsqrt(var + LN_EPS) * g + b


def _sigmoid(x):
    return 1.0 / (1.0 + jnp.exp(-x))


def _silu(x):
    return x * _sigmoid(x)


def _mm_kernel(x_ref, w_ref, o_ref, xb_ref):
    @pl.when(pl.program_id(1) == 0)
    def _():
        xb_ref[...] = x_ref[...].astype(BF16)

    o_ref[...] = _dot(xb_ref[...], w_ref[...])


def _mm(x, w, tm, tn, name):
    n, k = x.shape
    f = w.shape[1]
    return pl.pallas_call(
        _mm_kernel,
        grid=(n // tm, f // tn),
        in_specs=[pl.BlockSpec((tm, k), lambda i, j: (i, 0)),
                  pl.BlockSpec((k, tn), lambda i, j: (0, j))],
        out_specs=pl.BlockSpec((tm, tn), lambda i, j: (i, j)),
        out_shape=jax.ShapeDtypeStruct((n, f), F32),
        scratch_shapes=[pltpu.VMEM((tm, k), BF16)],
        compiler_params=_cp(("arbitrary", "arbitrary"), 48),
        name=name,
    )(x, w)


def _mm_res_ln_kernel(a_ref, w_ref, x_ref, g_ref, b_ref, o_ref):
    h = _dot(a_ref[...].astype(BF16), w_ref[...])
    o_ref[...] = _layernorm(DN_ALPHA * x_ref[...] + h, g_ref[...], b_ref[...])


def _mm_res_ln(a, w, x, g, b, tm, name):
    n, k = a.shape
    return pl.pallas_call(
        _mm_res_ln_kernel,
        grid=(n // tm,),
        in_specs=[pl.BlockSpec((tm, k), lambda i: (i, 0)),
                  pl.BlockSpec((k, D_MODEL), lambda i: (0, 0)),
                  pl.BlockSpec((tm, D_MODEL), lambda i: (i, 0)),
                  pl.BlockSpec((1, D_MODEL), lambda i: (0, 0)),
                  pl.BlockSpec((1, D_MODEL), lambda i: (0, 0))],
        out_specs=pl.BlockSpec((tm, D_MODEL), lambda i: (i, 0)),
        out_shape=jax.ShapeDtypeStruct((n, D_MODEL), F32),
        compiler_params=_cp(("arbitrary",), 48),
        name=name,
    )(a, w, x, g, b)


def _ab_in_kernel(x_ref, w_ref, c_ref, sa_ref, sb_ref,
                  qb_ref, kf_ref, kb_ref, vf_ref, vb_ref, cu_ref, gb_ref):
    y = _dot(x_ref[...].astype(BF16), w_ref[...])

    def tile4(t):
        t = t[...]
        return jnp.concatenate([t, t, t, t], axis=1)

    c, sa, sb = tile4(c_ref), tile4(sa_ref), tile4(sb_ref)

    def rope(t):
        return (t * c + pltpu.roll(t, AW - ROPE_DIMS // 2, 1) * sa
                + pltpu.roll(t, ROPE_DIMS // 2, 1) * sb)

    q = rope(y[:, 0:AW])
    k = rope(y[:, AW:2 * AW])
    v = y[:, 2 * AW:3 * AW]
    u = y[:, 3 * AW:3 * AW + SCONV_CH]
    gate_b = y[:, 3 * AW + SCONV_CH:3 * AW + 2 * SCONV_CH]
    gate_c = y[:, 3 * AW + 2 * SCONV_CH:]
    qb_ref[...] = (q * (HD_A ** -0.5)).astype(BF16)
    kf_ref[...] = k
    kb_ref[...] = k.astype(BF16)
    vf_ref[...] = v
    vb_ref[...] = v.astype(BF16)
    cu_ref[...] = gate_c * u
    gb_ref[...] = gate_b


def _ab_in(x, w, tabs, tm):
    n = x.shape[0]
    fw = w.shape[1]
    row = lambda i: (i, 0)
    half = pl.BlockSpec((tm, AW), row)
    tab = pl.BlockSpec((tm, 128), row)
    return pl.pallas_call(
        _ab_in_kernel,
        grid=(n // tm,),
        in_specs=[pl.BlockSpec((tm, D_MODEL), row),
                  pl.BlockSpec((D_MODEL, fw), lambda i: (0, 0)),
                  tab, tab, tab],
        out_specs=[half] * 7,
        out_shape=[jax.ShapeDtypeStruct((n, AW), BF16),
                   jax.ShapeDtypeStruct((n, AW), F32),
                   jax.ShapeDtypeStruct((n, AW), BF16),
                   jax.ShapeDtypeStruct((n, AW), F32),
                   jax.ShapeDtypeStruct((n, AW), BF16),
                   jax.ShapeDtypeStruct((n, AW), F32),
                   jax.ShapeDtypeStruct((n, AW), F32)],
        compiler_params=_cp(("arbitrary",), 48),
        name="ab_in",
    )(x, w, *tabs)


def _rope_tables(pos):
    half = ROPE_DIMS // 2
    inv_freq = ROPE_THETA ** (-2.0 * jnp.arange(half, dtype=F32) / ROPE_DIMS)
    ang = pos.astype(F32)[:, None] * inv_freq[None, :]
    cos, sin = jnp.cos(ang), jnp.sin(ang)
    n = pos.shape[0]
    rest1 = jnp.ones((n, HD_A - ROPE_DIMS), F32)
    rest0 = jnp.zeros((n, HD_A - ROPE_DIMS), F32)
    z = jnp.zeros((n, half), F32)
    c = jnp.concatenate([cos, cos, rest1], axis=1)
    sa = jnp.concatenate([-sin, z, rest0], axis=1)
    sb = jnp.concatenate([z, sin, rest0], axis=1)
    return tuple(jnp.concatenate([t, t], axis=1) for t in (c, sa, sb))


def _block_mean_kernel(k_ref, o_ref):
    o_ref[0] = jnp.sum(k_ref[...], axis=0, keepdims=True) * (1.0 / KBLK)


def _block_mean(k):
    n = k.shape[0]
    nb = n // KBLK
    out = pl.pallas_call(
        _block_mean_kernel,
        grid=(nb,),
        in_specs=[pl.BlockSpec((KBLK, AW), lambda i: (i, 0))],
        out_specs=pl.BlockSpec((1, 1, AW), lambda i: (i, 0, 0)),
        out_shape=jax.ShapeDtypeStruct((nb, 1, AW), F32),
        compiler_params=_cp(("arbitrary",), 32),
        name="moba_block_mean",
    )(k)
    return out.reshape(nb, AW)


def _top3_mask(g, idx, n_idx, axis):
    sel = jnp.zeros(g.shape, jnp.bool_)
    for _ in range(TOPK_A):
        mx = jnp.max(g, axis=axis, keepdims=True)
        first = jnp.min(jnp.where(g == mx, idx, n_idx), axis=axis, keepdims=True)
        pick = idx == first
        sel = jnp.logical_or(sel, pick)
        g = jnp.where(pick, -jnp.inf, g)
    return sel


def _moba_prompt_kernel(q_ref, k_ref, vt2_ref, vto_ref, km_ref, o_ref, bias_ref, s0_ref, s1_ref):
    i = pl.program_id(1)
    nb = km_ref.shape[0]
    q = q_ref[...]
    km = km_ref[...].astype(BF16)
    lane = lax.broadcasted_iota(jnp.int32, q.shape, 1)
    blk = lax.broadcasted_iota(jnp.int32, (nb, KBLK), 0)
    kpos = lax.broadcasted_iota(jnp.int32, (KBLK, KBLK), 0)
    qpos = lax.broadcasted_iota(jnp.int32, (KBLK, KBLK), 1)
    k_own = k_ref[pl.ds(pl.multiple_of(i * KBLK, KBLK), KBLK), :]
    qhs, init = [], []
    for h in range(2):
        qh = jnp.where(lane // HD_A == h, q, jnp.zeros_like(q))
        qhs.append(qh)
        gate = _dot_nt(km, qh)
        gate = jnp.where(blk < i, gate, -jnp.inf)
        sel = jnp.logical_and(_top3_mask(gate, blk, nb, 0), blk < i)
        bias_ref[h, 0:nb, :] = jnp.where(sel, 0.0, NEG_BIG)
        bias_ref[h, nb:nb + SUBLANES, :] = jnp.full((SUBLANES, KBLK), NEG_BIG, F32)
        s = _dot_nt(k_own, qh)
        s = jnp.where(kpos <= qpos, s, NEG_BIG)
        m = jnp.max(s, axis=0, keepdims=True)
        p = jnp.exp(s - m)
        l = jnp.sum(p, axis=0, keepdims=True)
        acc = _dot(vto_ref[0, h * HD_A:(h + 1) * HD_A, :], p.astype(BF16))
        init += [m, l, acc]

    last_pair = nb // 2 - 1

    def qk_into(dst_ref, pair):
        pc = jnp.minimum(pair, last_pair)
        k2 = k_ref[pl.ds(pl.multiple_of(pc * 2 * KBLK, 2 * KBLK), 2 * KBLK), :]
        for h in range(2):
            dst_ref[h] = _dot_nt(k2, qhs[h])

    def consume(src_ref, pair, carry):
        pc = jnp.minimum(pair, last_pair)
        out = []
        for h in range(2):
            m, l, acc = carry[3 * h:3 * h + 3]
            sa = src_ref[h, 0:KBLK, :] + bias_ref[h, pl.ds(2 * pair, 1), :]
            sb = src_ref[h, KBLK:2 * KBLK, :] + bias_ref[h, pl.ds(2 * pair + 1, 1), :]
            m_new = jnp.maximum(m, jnp.maximum(jnp.max(sa, axis=0, keepdims=True),
                                               jnp.max(sb, axis=0, keepdims=True)))
            a = jnp.exp(m - m_new)
            pa = jnp.exp(sa - m_new)
            pb = jnp.exp(sb - m_new)
            l = a * l + jnp.sum(pa, axis=0, keepdims=True) + jnp.sum(pb, axis=0, keepdims=True)
            p2 = jnp.concatenate([pa, pb], axis=0).astype(BF16)
            acc = a * acc + _dot(vt2_ref[pc, h * HD_A:(h + 1) * HD_A, :], p2)
            out += [m_new, l, acc]
        return tuple(out)

    def body(t, carry):
        qk_into(s1_ref, 2 * t + 1)
        carry = consume(s0_ref, 2 * t, carry)
        qk_into(s0_ref, 2 * t + 2)
        return consume(s1_ref, 2 * t + 1, carry)

    npairs = (i + 1) // 2
    qk_into(s0_ref, 0)
    res = lax.fori_loop(0, (npairs + 1) // 2, body, tuple(init))
    outs = [res[2] / res[1], res[5] / res[4]]
    o_ref[...] = jnp.concatenate(outs, axis=0).T


def _moba_prompt(qb, kb, vb, km):
    n = qb.shape[0]
    nb = n // KBLK
    assert nb % 2 == 0
    pairs = AW // 128
    vt_own = vb.reshape(nb, KBLK, AW).transpose(0, 2, 1)
    vt_two = vb.reshape(nb // 2, 2 * KBLK, AW).transpose(0, 2, 1)
    return pl.pallas_call(
        _moba_prompt_kernel,
        grid=(pairs, nb),
        in_specs=[pl.BlockSpec((KBLK, 128), lambda p, i: (i, p)),
                  pl.BlockSpec((n, 128), lambda p, i: (0, p)),
                  pl.BlockSpec((nb // 2, 128, 2 * KBLK), lambda p, i: (0, p, 0)),
                  pl.BlockSpec((1, 128, KBLK), lambda p, i: (i, p, 0)),
                  pl.BlockSpec((nb, 128), lambda p, i: (0, p))],
        out_specs=pl.BlockSpec((KBLK, 128), lambda p, i: (i, p)),
        out_shape=jax.ShapeDtypeStruct((n, AW), F32),
        scratch_shapes=[pltpu.VMEM((2, nb + SUBLANES, KBLK), F32),
                        pltpu.VMEM((2, 2 * KBLK, KBLK), F32),
                        pltpu.VMEM((2, 2 * KBLK, KBLK), F32)],
        compiler_params=_cp(("arbitrary", "arbitrary"), 48),
        name="moba_prompt",
    )(qb, kb, vt_two, vt_own, km)


def _moba_sample_kernel(pt_ref, q_ref, kn_ref, vn_ref, *rest, nblk, tq):
    del pt_ref
    npg = SAMPLE_PAGES_PER_STEP
    k_refs, v_refs = rest[:npg], rest[npg:2 * npg]
    o_ref, qexp_ref, gate_ref, s_ref, acc_ref, l_ref = rest[2 * npg:]
    j = pl.program_id(1)
    nsteps = 2 * nblk // npg
    bps = npg // 2

    @pl.when(j == 0)
    def _():
        q = q_ref[...]
        lane = lax.broadcasted_iota(jnp.int32, q.shape, 1)
        qexp_ref[...] = jnp.concatenate(
            [jnp.where(lane // HD_A == h, q, jnp.zeros_like(q)) for h in range(HEADS_A)], axis=0)
        gate_ref[...] = jnp.full(gate_ref.shape, -jnp.inf, F32)

    @pl.when(j < nsteps)
    def _():
        qe = qexp_ref[...]
        bl = lax.broadcasted_iota(jnp.int32, gate_ref.shape, 1)
        pages = [r[0] for r in k_refs]
        kms = [jnp.sum(pages[2 * bi] + pages[2 * bi + 1], axis=1, keepdims=True) * (1.0 / KBLK)
               for bi in range(bps)]
        rhs = jnp.concatenate([p.astype(BF16) for p in pages]
                              + [jnp.broadcast_to(km, (AW, 128)).astype(BF16) for km in kms], axis=1)
        s_all = _dot(qe, rhs)
        gate = gate_ref[...]
        for bi in range(bps):
            blk = j * bps + bi
            s_ref[blk] = s_all[:, bi * KBLK:(bi + 1) * KBLK]
            g = s_all[:, (npg + bi) * 128:(npg + bi + 1) * 128]
            gate = jnp.where(bl == blk, g, gate)
        gate_ref[...] = gate

    @pl.when(j == nsteps - 1)
    def _():
        qe = qexp_ref[...]
        gate = gate_ref[...]
        bl = lax.broadcasted_iota(jnp.int32, gate.shape, 1)
        self_f = jnp.logical_and(_top3_mask(gate, bl, gate.shape[1], 1), bl < nblk).astype(F32)

        s_own = _dot_nt(qe, kn_ref[0])
        rq = lax.broadcasted_iota(jnp.int32, s_own.shape, 0) % tq
        kt = lax.broadcasted_iota(jnp.int32, s_own.shape, 1)
        s_own = jnp.where(kt <= rq, s_own, NEG_BIG)
        m = jnp.max(s_own, axis=1, keepdims=True)
        picked = [self_f[:, jj:jj + 1] > 0.0 for jj in range(nblk)]
        for jj in range(nblk):
            sj = jnp.where(picked[jj], s_ref[jj], NEG_BIG)
            m = jnp.maximum(m, jnp.max(sj, axis=1, keepdims=True))
        p_own = jnp.exp(s_own - m)
        l = jnp.sum(p_own, axis=1, keepdims=True)
        for jj in range(nblk):
            pj = jnp.where(picked[jj], jnp.exp(s_ref[jj] - m), 0.0)
            s_ref[jj] = pj
            l = l + jnp.sum(pj, axis=1, keepdims=True)
        l_ref[...] = jnp.broadcast_to(l, l_ref.shape)
        acc_ref[...] = _dot(p_own.astype(BF16), vn_ref[0])

    @pl.when(j >= nsteps)
    def _():
        p = jnp.concatenate([s_ref[(j - nsteps) * bps + bi] for bi in range(bps)], axis=1).astype(BF16)
        vt = jnp.concatenate([r[0].astype(BF16) for r in v_refs], axis=1)
        acc_ref[...] += _dot_nt(p, vt)

    @pl.when(j == 2 * nsteps - 1)
    def _():
        o = acc_ref[...] / l_ref[:, 0:1]
        lane = lax.broadcasted_iota(jnp.int32, (tq, AW), 1)
        res = jnp.zeros((tq, AW), F32)
        for h in range(HEADS_A):
            res = res + jnp.where(lane // HD_A == h, o[h * tq:(h + 1) * tq, :], 0.0)
        o_ref[...] = res


def _moba_sample(qb, kb_new, vb_new, cache_k, cache_v, page_table, tq):
    n = qb.shape[0]
    nseq = n // tq
    npages = page_table.shape[1]
    nblk = npages * PAGE // KBLK
    npg = SAMPLE_PAGES_PER_STEP
    assert KBLK == 2 * PAGE and tq <= 128 and nblk <= 128 and npages % npg == 0
    nsteps = npages // npg
    ck = cache_k.transpose(0, 2, 3, 1).reshape(cache_k.shape[0], AW, PAGE)
    cv = cache_v.transpose(0, 2, 3, 1).reshape(cache_v.shape[0], AW, PAGE)
    pad = ((0, 0), (0, 128 - tq), (0, 0))
    kn = jnp.pad(kb_new.reshape(nseq, tq, AW), pad)
    vn = jnp.pad(vb_new.reshape(nseq, tq, AW), pad)
    rows = HEADS_A * tq

    def kpage(off):
        return pl.BlockSpec((1, AW, PAGE),
                            lambda b, j, pt: (pt[b, npg * jnp.minimum(j, nsteps - 1) + off], 0, 0))

    def vpage(off):
        return pl.BlockSpec((1, AW, PAGE),
                            lambda b, j, pt: (pt[b, npg * jnp.maximum(j - nsteps, 0) + off], 0, 0))

    seq3 = pl.BlockSpec((1, 128, AW), lambda b, j, pt: (b, 0, 0))
    grid_spec = pltpu.PrefetchScalarGridSpec(
        num_scalar_prefetch=1,
        grid=(nseq, 2 * nsteps),
        in_specs=([pl.BlockSpec((tq, AW), lambda b, j, pt: (b, 0)), seq3, seq3]
                  + [kpage(off) for off in range(npg)] + [vpage(off) for off in range(npg)]),
        out_specs=pl.BlockSpec((tq, AW), lambda b, j, pt: (b, 0)),
        scratch_shapes=[pltpu.VMEM((rows, AW), BF16),
                        pltpu.VMEM((rows, 128), F32),
                        pltpu.VMEM((nblk, rows, KBLK), F32),
                        pltpu.VMEM((rows, AW), F32),
                        pltpu.VMEM((rows, 128), F32)])
    return pl.pallas_call(
        functools.partial(_moba_sample_kernel, nblk=nblk, tq=tq),
        grid_spec=grid_spec,
        out_shape=jax.ShapeDtypeStruct((n, AW), F32),
        compiler_params=_cp(("arbitrary", "arbitrary"), 32),
        name="moba_sample",
    )(page_table, qb, kn, vn, *([ck] * npg), *([cv] * npg))


def _ab_out_kernel(attn_ref, cu_ref, cup_ref, hist_ref, gb_ref, x_ref, w_ref, cw_ref, g_ref, b_ref,
                   o_ref, hbuf_ref, *, tt):
    j = pl.program_id(1)
    hbuf_ref[0:SUBLANES, :] = jnp.where(j == 0, hist_ref[0], cup_ref[...])
    hbuf_ref[SUBLANES:, :] = cu_ref[...]
    cw = cw_ref[...]
    conv = jnp.zeros((tt, SCONV_CH), F32)
    for tap in range(SCONV_W):
        r0 = SUBLANES - (SCONV_W - 1) + tap
        conv = conv + hbuf_ref[r0:r0 + tt, :] * cw[tap:tap + 1, :]
    conv = gb_ref[...] * conv
    mix = jnp.concatenate([attn_ref[...], conv], axis=1).astype(BF16)
    h = _dot(mix, w_ref[...])
    o_ref[...] = _layernorm(DN_ALPHA * x_ref[...] + h, g_ref[...], b_ref[...])


def _ab_out(attn, cu, hist, gb, x, w, cw, g, b, nseq, tt):
    n = x.shape[0]
    t = n // nseq
    nt = t // tt
    hist8 = jnp.pad(hist, ((0, 0), (SUBLANES - hist.shape[1], 0), (0, 0)))
    row = lambda s, j: (s * nt + j, 0)
    prev = lambda s, j: (jnp.maximum((s * t + j * tt) // SUBLANES - 1, 0), 0)
    const = lambda s, j: (0, 0)
    return pl.pallas_call(
        functools.partial(_ab_out_kernel, tt=tt),
        grid=(nseq, nt),
        in_specs=[pl.BlockSpec((tt, AW), row),
                  pl.BlockSpec((tt, SCONV_CH), row),
                  pl.BlockSpec((SUBLANES, SCONV_CH), prev),
                  pl.BlockSpec((1, SUBLANES, SCONV_CH), lambda s, j: (s, 0, 0)),
                  pl.BlockSpec((tt, SCONV_CH), row),
                  pl.BlockSpec((tt, D_MODEL), row),
                  pl.BlockSpec((AW + SCONV_CH, D_MODEL), const),
                  pl.BlockSpec((SCONV_W, SCONV_CH), const),
                  pl.BlockSpec((1, D_MODEL), const),
                  pl.BlockSpec((1, D_MODEL), const)],
        out_specs=pl.BlockSpec((tt, D_MODEL), row),
        out_shape=jax.ShapeDtypeStruct((n, D_MODEL), F32),
        scratch_shapes=[pltpu.VMEM((tt + SUBLANES, SCONV_CH), F32)],
        compiler_params=_cp(("arbitrary", "arbitrary"), 48),
        name="ab_out",
    )(attn, cu, cu, hist8, gb, x, w, cw, g, b)


def _mem_head(ref, h):
    if ref.shape[1] == MEM_LEN:
        return ref[0, :, h * MEM_HD:(h + 1) * MEM_HD]
    chunks = MEM_HD // 128
    return jnp.concatenate([ref[0, pl.ds(c * MEM_HEADS + h, MEM_LEN, stride=chunks * MEM_HEADS), :]
                            for c in range(chunks)], axis=1)


def _memattn_kernel(x_ref, wq_ref, mk_ref, mv_ref, wo_ref, g_ref, b_ref, o_ref):
    x = x_ref[...]
    q = _dot(x.astype(BF16), wq_ref[...])
    outs = []
    for h in range(MEM_HEADS):
        sl = slice(h * MEM_HD, (h + 1) * MEM_HD)
        s = _dot_nt(q[:, sl].astype(BF16), _mem_head(mk_ref, h).astype(BF16)) * (MEM_HD ** -0.5)
        s = s - jnp.max(s, axis=-1, keepdims=True)
        p = jnp.exp(s)
        p = p / jnp.sum(p, axis=-1, keepdims=True)
        outs.append(_dot(p.astype(BF16), _mem_head(mv_ref, h).astype(BF16)))
    o = jnp.concatenate(outs, axis=1).astype(BF16)
    y = _dot(o, wo_ref[...])
    o_ref[...] = _layernorm(DN_ALPHA * x + y, g_ref[...], b_ref[...])


def _memattn(x, wq, mk, mv, wo, g, b, tm, mem_base, nb):
    n = x.shape[0]
    per = n // tm // nb
    row = lambda i: (i, 0)
    const = lambda i: (0, 0)
    mem = pl.BlockSpec((1,) + mk.shape[1:], lambda i: (mem_base + i // per, 0, 0))
    return pl.pallas_call(
        _memattn_kernel,
        grid=(n // tm,),
        in_specs=[pl.BlockSpec((tm, D_MODEL), row),
                  pl.BlockSpec((D_MODEL, D_MODEL), const),
                  mem, mem,
                  pl.BlockSpec((D_MODEL, D_MODEL), const),
                  pl.BlockSpec((1, D_MODEL), const),
                  pl.BlockSpec((1, D_MODEL), const)],
        out_specs=pl.BlockSpec((tm, D_MODEL), row),
        out_shape=jax.ShapeDtypeStruct((n, D_MODEL), F32),
        compiler_params=_cp(("arbitrary",), 48),
        name="memattn",
    )(x, wq, mk, mv, wo, g, b)


def _route(logits, bias):
    lane = lax.broadcasted_iota(jnp.int32, logits.shape, 1)
    grp = lane // EXP_PER_GROUP
    ex = jnp.exp(logits - jnp.max(logits, axis=1, keepdims=True))
    scores = ex / jnp.sum(ex, axis=1, keepdims=True)
    biased = scores + bias

    def top2(vals):
        t1 = jnp.max(vals, axis=1, keepdims=True)
        i1 = jnp.min(jnp.where(vals == t1, lane, N_EXPERTS), axis=1, keepdims=True)
        rest = jnp.where(lane == i1, -jnp.inf, vals)
        t2 = jnp.max(rest, axis=1, keepdims=True)
        i2 = jnp.min(jnp.where(rest == t2, lane, N_EXPERTS), axis=1, keepdims=True)
        return t1, i1, t2, i2

    best = jnp.zeros((logits.shape[0], 1), jnp.int32)
    best_v = None
    for gi in range(N_GROUPS):
        t1, _, t2, _ = top2(jnp.where(grp == gi, biased, -jnp.inf))
        gs = t1 + t2
        if best_v is None:
            best_v = gs
        else:
            better = gs > best_v
            best = jnp.where(better, gi, best)
            best_v = jnp.where(better, gs, best_v)
    _, i1, _, i2 = top2(jnp.where(grp == best, biased, -jnp.inf))
    s1 = jnp.sum(jnp.where(lane == i1, scores, 0.0), axis=1, keepdims=True)
    s2 = jnp.sum(jnp.where(lane == i2, scores, 0.0), axis=1, keepdims=True)
    den = s1 + s2
    return jnp.where(lane == i1, s1 / den, 0.0) + jnp.where(lane == i2, s2 / den, 0.0)


def _moe_kernel(x_ref, wr_ref, rb_ref, wgu_ref, wd_ref, g_ref, b_ref, o_ref,
                xb_ref, comb_ref, acc_ref):
    e = pl.program_id(1)

    @pl.when(e == 0)
    def _():
        x = x_ref[...]
        xb_ref[...] = x.astype(BF16)
        comb_ref[...] = _route(_dot_bf(x, wr_ref[...]), rb_ref[...])
        acc_ref[...] = jnp.zeros_like(acc_ref)

    hgu = _dot(xb_ref[...], wgu_ref[0])
    comb = comb_ref[...]
    lane = lax.broadcasted_iota(jnp.int32, comb.shape, 1)
    ce = jnp.sum(jnp.where(lane == e, comb, 0.0), axis=1, keepdims=True)
    h = _silu(hgu[:, :EXPERT_FF]) * hgu[:, EXPERT_FF:] * ce
    acc_ref[...] += _dot(h.astype(BF16), wd_ref[0])

    @pl.when(e == N_EXPERTS - 1)
    def _():
        o_ref[...] = _layernorm(DN_ALPHA * x_ref[...] + acc_ref[...], g_ref[...], b_ref[...])


def _moe(x, wr, rb, wgu, wd, g, b, tm):
    n = x.shape[0]
    row = lambda i, e: (i, 0)
    const = lambda i, e: (0, 0)
    return pl.pallas_call(
        _moe_kernel,
        grid=(n // tm, N_EXPERTS),
        in_specs=[pl.BlockSpec((tm, D_MODEL), row),
                  pl.BlockSpec((D_MODEL, N_EXPERTS), const),
                  pl.BlockSpec((1, N_EXPERTS), const),
                  pl.BlockSpec((1, D_MODEL, 2 * EXPERT_FF), lambda i, e: (e, 0, 0)),
                  pl.BlockSpec((1, EXPERT_FF, D_MODEL), lambda i, e: (e, 0, 0)),
                  pl.BlockSpec((1, D_MODEL), const),
                  pl.BlockSpec((1, D_MODEL), const)],
        out_specs=pl.BlockSpec((tm, D_MODEL), row),
        out_shape=jax.ShapeDtypeStruct((n, D_MODEL), F32),
        scratch_shapes=[pltpu.VMEM((tm, D_MODEL), BF16),
                        pltpu.VMEM((tm, N_EXPERTS), F32),
                        pltpu.VMEM((tm, D_MODEL), F32)],
        compiler_params=_cp(("arbitrary", "arbitrary"), 48),
        name="moe",
    )(x, wr, rb, wgu, wd, g, b)


def _gdn_prep_kernel(qkv_ref, prev_ref, hist_ref, cw_ref, x_ref, wa_ref, wb_ref, alog_ref, dtb_ref,
                     qn_ref, kn_ref, v_ref, g_ref, beta_ref, hbuf_ref, *, tt):
    j = pl.program_id(1)
    hbuf_ref[0:SUBLANES, :] = jnp.where(j == 0, hist_ref[0], prev_ref[...])
    hbuf_ref[SUBLANES:, :] = qkv_ref[...]
    for c in range(G_QKV_DIM // 128):
        sl = slice(c * 128, (c + 1) * 128)
        y = jnp.zeros((tt, 128), F32)
        for tap in range(G_CONV_W):
            r0 = SUBLANES - (G_CONV_W - 1) + tap
            y = y + hbuf_ref[r0:r0 + tt, sl] * cw_ref[tap:tap + 1, sl]
        y = _silu(y)
        if c < 2 * G_QK_HEADS:
            y = y * lax.rsqrt(jnp.sum(y * y, axis=-1, keepdims=True) + RMS_EPS)
            if c < G_QK_HEADS:
                qn_ref[:, sl] = y * (G_DK ** -0.5)
            else:
                kn_ref[:, (c - G_QK_HEADS) * 128:(c - G_QK_HEADS + 1) * 128] = y
        else:
            v_ref[:, (c - 2 * G_QK_HEADS) * 128:(c - 2 * G_QK_HEADS + 1) * 128] = y
    x = x_ref[...]
    a = _dot_bf(x, wa_ref[...]) + dtb_ref[...]
    softplus = jnp.maximum(a, 0.0) + jnp.log(1.0 + jnp.exp(-jnp.abs(a)))
    g_ref[...] = -jnp.exp(alog_ref[...]) * softplus
    beta_ref[...] = _sigmoid(_dot_bf(x, wb_ref[...]))


def _gdn_prep(big, hist, cw, x, wa, wb, alog, dtb, nseq, tt):
    n = x.shape[0]
    t = n // nseq
    nt = t // tt
    hist8 = jnp.pad(hist, ((0, 0), (SUBLANES - hist.shape[1], 0), (0, 0)))
    row = lambda s, j: (s * nt + j, 0)
    prev = lambda s, j: (jnp.maximum((s * t + j * tt) // SUBLANES - 1, 0), 0)
    const = lambda s, j: (0, 0)
    hv = G_V_HEADS
    return pl.pallas_call(
        functools.partial(_gdn_prep_kernel, tt=tt),
        grid=(nseq, nt),
        in_specs=[pl.BlockSpec((tt, G_QKV_DIM), row),
                  pl.BlockSpec((SUBLANES, G_QKV_DIM), prev),
                  pl.BlockSpec((1, SUBLANES, G_QKV_DIM), lambda s, j: (s, 0, 0)),
                  pl.BlockSpec((G_CONV_W, G_QKV_DIM), const),
                  pl.BlockSpec((tt, D_MODEL), row),
                  pl.BlockSpec((D_MODEL, hv), const),
                  pl.BlockSpec((D_MODEL, hv), const),
                  pl.BlockSpec((1, hv), const),
                  pl.BlockSpec((1, hv), const)],
        out_specs=[pl.BlockSpec((tt, G_QK_DIM), row),
                   pl.BlockSpec((tt, G_QK_DIM), row),
                   pl.BlockSpec((tt, G_V_DIM), row),
                   pl.BlockSpec((tt, hv), row),
                   pl.BlockSpec((tt, hv), row)],
        out_shape=[jax.ShapeDtypeStruct((n, G_QK_DIM), F32),
                   jax.ShapeDtypeStruct((n, G_QK_DIM), F32),
                   jax.ShapeDtypeStruct((n, G_V_DIM), F32),
                   jax.ShapeDtypeStruct((n, hv), F32),
                   jax.ShapeDtypeStruct((n, hv), F32)],
        scratch_shapes=[pltpu.VMEM((tt + SUBLANES, G_QKV_DIM), F32)],
        compiler_params=_cp(("arbitrary", "arbitrary"), 48),
        name="gdn_prep",
    )(big, big, hist8, cw, x, wa, wb, alog, dtb)


def _gdn_geometry(c):
    hpg = min(G_V_HEADS, MXU_DIM // c)
    hpt = 128 // c
    return hpg, G_V_HEADS // hpg, hpt, G_V_HEADS // hpt


def _gdn_local_kernel(q_ref, k_ref, v_ref, g_ref, beta_ref,
                      u_ref, wq_ref, qk_ref, kdt_ref, egl_ref, *, c):
    hpg, ngroups, _, _ = _gdn_geometry(c)
    r = hpg * c
    rep = G_V_HEADS // G_QK_HEADS
    ri = lax.broadcasted_iota(jnp.int32, (c, c), 0)
    ci = lax.broadcasted_iota(jnp.int32, (c, c), 1)
    tri = (ri >= ci).astype(BF16)
    g0, g1, g2 = _split3(g_ref[...])
    gcum = _dot(tri, g0) + _dot(tri, g1) + _dot(tri, g2)
    beta = beta_ref[...]
    egl_ref[0] = jnp.exp(gcum[c - 1:c, :])

    rr = lax.broadcasted_iota(jnp.int32, (r, r), 0)
    cr = lax.broadcasted_iota(jnp.int32, (r, r), 1)
    same = (rr // c) == (cr // c)
    incl = jnp.logical_and(same, rr >= cr)
    strict = jnp.logical_and(same, rr > cr)
    eye = (rr == cr).astype(F32)

    def stack(fn, heads):
        return jnp.concatenate([fn(h) for h in heads], axis=0)

    groups = [list(range(gi * hpg, (gi + 1) * hpg)) for gi in range(ngroups)]
    k_st, q_st, kb_st, gc_col, decay, x, tinv = [], [], [], [], [], [], []
    for heads in groups:
        ks = stack(lambda h: k_ref[:, (h // rep) * G_DK:(h // rep + 1) * G_DK], heads)
        bt = stack(lambda h: beta[:, h:h + 1], heads)
        gc = stack(lambda h: gcum[:, h:h + 1], heads)
        gmat = jnp.broadcast_to(gc, (r, r))
        dec = jnp.exp(jnp.where(incl, gmat - gmat.T, -jnp.inf))
        kb = ks * bt
        xg = -jnp.where(strict, _dot_nt(kb.astype(BF16), ks.astype(BF16)) * dec, 0.0)
        k_st.append(ks); kb_st.append(kb); gc_col.append(gc); decay.append(dec)
        x.append(xg); tinv.append(eye + xg)
    for _ in range(int(math.log2(c)) - 1):
        x = [_dot(xg.astype(BF16), xg.astype(BF16)) for xg in x]
        tinv = [tg + _dot(tg.astype(BF16), xg.astype(BF16)) for tg, xg in zip(tinv, x)]
    for gi, heads in enumerate(groups):
        ks, kb, gc, dec = k_st[gi], kb_st[gi], gc_col[gi], decay[gi]
        tb = tinv[gi].astype(BF16)
        bt = stack(lambda h: beta[:, h:h + 1], heads)
        vs = stack(lambda h: v_ref[:, h * G_DV:(h + 1) * G_DV], heads)
        qs = stack(lambda h: q_ref[:, (h // rep) * G_DK:(h // rep + 1) * G_DK], heads)
        eg = jnp.exp(gc)
        u_st = _dot(tb, (vs * bt).astype(BF16))
        w_st = _dot(tb, (kb * eg).astype(BF16))
        qg_st = qs * eg
        qk_ref[0, gi] = (_dot_nt(qs.astype(BF16), ks.astype(BF16)) * dec).astype(BF16)
        gl = stack(lambda h: jnp.broadcast_to(gcum[c - 1:c, h:h + 1], (c, 1)), heads)
        kdec = ks * jnp.exp(gl - gc)
        for tt in range(r // 128):
            kdt_ref[0, gi * (r // 128) + tt] = kdec[tt * 128:(tt + 1) * 128, :].T.astype(BF16)
        for hl, h in enumerate(heads):
            rows = slice(hl * c, (hl + 1) * c)
            u_ref[:, h * G_DV:(h + 1) * G_DV] = u_st[rows]
            wq_ref[0, h] = jnp.concatenate([w_st[rows], qg_st[rows]], axis=0).astype(BF16)


def _gdn_local(qn, kn, v, g, beta, c):
    n = qn.shape[0]
    nc = n // c
    hv = G_V_HEADS
    hpg, ngroups, _, ntiles = _gdn_geometry(c)
    r = hpg * c
    row = lambda i: (i, 0)
    blk4 = lambda i: (i, 0, 0, 0)
    return pl.pallas_call(
        functools.partial(_gdn_local_kernel, c=c),
        grid=(nc,),
        in_specs=[pl.BlockSpec((c, G_QK_DIM), row),
                  pl.BlockSpec((c, G_QK_DIM), row),
                  pl.BlockSpec((c, G_V_DIM), row),
                  pl.BlockSpec((c, hv), row),
                  pl.BlockSpec((c, hv), row)],
        out_specs=[pl.BlockSpec((c, G_V_DIM), row),
                   pl.BlockSpec((1, hv, 2 * c, G_DK), blk4),
                   pl.BlockSpec((1, ngroups, r, r), blk4),
                   pl.BlockSpec((1, ntiles, 128, 128), blk4),
                   pl.BlockSpec((1, 1, hv), lambda i: (i, 0, 0))],
        out_shape=[jax.ShapeDtypeStruct((n, G_V_DIM), F32),
                   jax.ShapeDtypeStruct((nc, hv, 2 * c, G_DK), BF16),
                   jax.ShapeDtypeStruct((nc, ngroups, r, r), BF16),
                   jax.ShapeDtypeStruct((nc, ntiles, 128, 128), BF16),
                   jax.ShapeDtypeStruct((nc, 1, hv), F32)],
        compiler_params=_cp(("arbitrary",), 48),
        name="gdn_local",
    )(qn, kn, v, g, beta)


def _gdn_scan_kernel(u_ref, wq_ref, qk_ref, kdt_ref, egl_ref, z_ref, s0_ref, nw_ref,
                     o_ref, sfin_ref, s_scr, *, c, nchunks):
    j = pl.program_id(1)

    @pl.when(j == 0)
    def _():
        s_scr[...] = s0_ref[0]

    hpg, ngroups, hpt, ntiles = _gdn_geometry(c)
    egl = egl_ref[0]
    nw = nw_ref[...]
    hv = G_V_HEADS
    s_all = [s_scr[h] for h in range(hv)]
    ws = [_dot(wq_ref[0, h], s_all[h].astype(BF16)) for h in range(hv)]
    vnew = [u_ref[:, h * G_DV:(h + 1) * G_DV] - ws[h][0:c] for h in range(hv)]
    for gi in range(ngroups):
        heads = range(gi * hpg, (gi + 1) * hpg)
        vn_st = jnp.concatenate([vnew[h] for h in heads], axis=0).astype(BF16)
        o_st = jnp.concatenate([ws[h][c:2 * c] for h in heads], axis=0) + _dot(qk_ref[0, gi], vn_st)
        for hl, h in enumerate(heads):
            o = o_st[hl * c:(hl + 1) * c]
            zf = z_ref[:, h * G_DV:(h + 1) * G_DV]
            o = o * lax.rsqrt(jnp.mean(o * o, axis=-1, keepdims=True) + RMS_EPS) * nw * _silu(zf)
            o_ref[:, h * G_DV:(h + 1) * G_DV] = o
    row_head = lax.broadcasted_iota(jnp.int32, (128, G_DV), 0) // c
    for tt in range(ntiles):
        heads = range(tt * hpt, (tt + 1) * hpt)
        vn_t = jnp.concatenate([vnew[h] for h in heads], axis=0)
        vbd = jnp.concatenate([jnp.where(row_head == hl, vn_t, 0.0) for hl in range(hpt)],
                              axis=1).astype(BF16)
        upd = _dot(kdt_ref[0, tt], vbd)
        for hl, h in enumerate(heads):
            s_scr[h] = s_all[h] * egl[:, h:h + 1] + upd[:, hl * G_DV:(hl + 1) * G_DV]

    @pl.when(j == nchunks - 1)
    def _():
        sfin_ref[0] = s_scr[...]


def _gdn_scan(u, wq, qk, kdt, egl, big, s0, nw, nseq, c):
    n = u.shape[0]
    t = n // nseq
    nchunks = t // c
    hv = G_V_HEADS
    hpg, ngroups, _, ntiles = _gdn_geometry(c)
    r = hpg * c
    row = lambda s, j: (s * nchunks + j, 0)
    blk4 = lambda s, j: (s * nchunks + j, 0, 0, 0)
    zcol = G_QKV_DIM // G_V_DIM
    return pl.pallas_call(
        functools.partial(_gdn_scan_kernel, c=c, nchunks=nchunks),
        grid=(nseq, nchunks),
        in_specs=[pl.BlockSpec((c, G_V_DIM), row),
                  pl.BlockSpec((1, hv, 2 * c, G_DK), blk4),
                  pl.BlockSpec((1, ngroups, r, r), blk4),
                  pl.BlockSpec((1, ntiles, 128, 128), blk4),
                  pl.BlockSpec((1, 1, hv), lambda s, j: (s * nchunks + j, 0, 0)),
                  pl.BlockSpec((c, G_V_DIM), lambda s, j: (s * nchunks + j, zcol)),
                  pl.BlockSpec((1, hv, G_DK, G_DV), lambda s, j: (s, 0, 0, 0)),
                  pl.BlockSpec((1, G_DV), lambda s, j: (0, 0))],
        out_specs=[pl.BlockSpec((c, G_V_DIM), row),
                   pl.BlockSpec((1, hv, G_DK, G_DV), lambda s, j: (s, 0, 0, 0))],
        out_shape=[jax.ShapeDtypeStruct((n, G_V_DIM), F32),
                   jax.ShapeDtypeStruct((nseq, hv, G_DK, G_DV), F32)],
        scratch_shapes=[pltpu.VMEM((hv, G_DK, G_DV), F32)],
        compiler_params=_cp(("arbitrary", "arbitrary"), 48),
        name="gdn_scan",
    )(u, wq, qk, kdt, egl, big, s0, nw)


def _trunk(x, nseq, pos, moba_fn, sconv_hist, gdn_s0, gdn_hist, mem_k, mem_v, wts, tiles):
    n = x.shape[0]
    t = n // nseq
    assert t >= G_CONV_W - 1 and t >= SCONV_W - 1
    row = lambda a: a.reshape(1, -1)
    ln_g, ln_b = wts["ln_g"], wts["ln_b"]

    def mem_moe(x, l):
        nmem = mem_k.shape[0] // DEPTH
        x = _memattn(x, wts["mem_wq"][l], mem_k, mem_v, wts["mem_wo"][l],
                     row(ln_g[l, 1]), row(ln_b[l, 1]), tiles["mem"], l * nmem, nmem)
        return _moe(x, wts["w_router"], row(wts["router_bias"]), wts["moe_wgu"][l], wts["moe_wd"][l],
                    row(ln_g[l, 2]), row(ln_b[l, 2]), tiles["moe"])

    qb, kf, kb, vf, vb, cu, gb = _ab_in(x, wts["w_in_ab"], _rope_tables(pos), tiles["ab_in"])
    attn = moba_fn(qb, kf, kb, vb)
    x = _ab_out(attn, cu, sconv_hist, gb, x, wts["w_out_ab"], wts["sconv_w"],
                row(ln_g[0, 0]), row(ln_b[0, 0]), nseq, tiles["ab_out"])
    sconv_new = cu.reshape(nseq, t, SCONV_CH)[:, t - (SCONV_W - 1):]
    x = mem_moe(x, 0)

    big = _mm(x, wts["w_gdn_main"], tiles["mm"], 1536, "gdn_in")
    qn, kn, v, g, beta = _gdn_prep(big, gdn_hist, wts["gdn_conv_w"], x, wts["w_gdn_a"], wts["w_gdn_b"],
                                   row(wts["gdn_a_log"]), row(wts["gdn_dt_bias"]), nseq, tiles["gdn_prep"])
    c = min(G_CHUNK, t)
    u, wq, qk, kdt, egl = _gdn_local(qn, kn, v, g, beta, c)
    og, gdn_s = _gdn_scan(u, wq, qk, kdt, egl, big, gdn_s0, row(wts["gdn_norm_w"]), nseq, c)
    gdn_hist_new = big.reshape(nseq, t, -1)[:, t - (G_CONV_W - 1):, :G_QKV_DIM]
    x = _mm_res_ln(og, wts["w_out_gdn"], x, row(ln_g[1, 0]), row(ln_b[1, 0]), tiles["mm"], "gdn_out")
    x = mem_moe(x, 1)
    return x, kf, vf, sconv_new, gdn_s, gdn_hist_new


def kernel(x_prompt, x_sample, mem_prompt, cache_moba_k, cache_moba_v, page_table, state_sconv,
           state_gdn, state_gdn_conv, cache_mem_k, cache_mem_v, w_in_ab, w_out_ab, sconv_w, w_in_gdn,
           gdn_conv_w, gdn_a_log, gdn_dt_bias, gdn_norm_w, w_out_gdn, mem_wq, mem_wk, mem_wv, mem_wo,
           ln_g, ln_b, w_router, router_bias, moe_w_gate, moe_w_up, moe_w_down):
    b, seq, _ = x_prompt.shape
    db, dseq, _ = x_sample.shape
    assert b == 1 and seq % KBLK == 0
    zcols = G_QKV_DIM + G_V_DIM
    wts = dict(
        w_in_ab=w_in_ab.astype(BF16), w_out_ab=w_out_ab.astype(BF16), sconv_w=sconv_w,
        w_gdn_main=w_in_gdn[:, :zcols].astype(BF16),
        w_gdn_a=w_in_gdn[:, zcols:zcols + G_V_HEADS], w_gdn_b=w_in_gdn[:, zcols + G_V_HEADS:],
        gdn_conv_w=gdn_conv_w, gdn_a_log=gdn_a_log, gdn_dt_bias=gdn_dt_bias, gdn_norm_w=gdn_norm_w,
        w_out_gdn=w_out_gdn.astype(BF16),
        mem_wq=mem_wq.astype(BF16), mem_wo=mem_wo.astype(BF16),
        ln_g=ln_g, ln_b=ln_b, w_router=w_router, router_bias=router_bias,
        moe_wgu=jnp.concatenate([moe_w_gate, moe_w_up], axis=-1).astype(BF16),
        moe_wd=moe_w_down.astype(BF16))

    mw = MEM_HEADS * MEM_HD
    w_mem_kv = jnp.concatenate([mem_wk[l] for l in range(DEPTH)] + [mem_wv[l] for l in range(DEPTH)],
                               axis=1).astype(BF16)
    mem_kv = _mm(mem_prompt.reshape(MEM_LEN, D_MODEL), w_mem_kv, MEM_LEN, 1024, "mem_kv")
    mem_k_p = jnp.stack([mem_kv[:, l * mw:(l + 1) * mw] for l in range(DEPTH)]).reshape(DEPTH, b, MEM_LEN, mw)
    mem_v_p = jnp.stack([mem_kv[:, (DEPTH + l) * mw:(DEPTH + l + 1) * mw]
                         for l in range(DEPTH)]).reshape(DEPTH, b, MEM_LEN, mw)

    def moba_p(qb, kf, kb, vb):
        return _moba_prompt(qb, kb, vb, _block_mean(kf))

    tiles_p = dict(ab_in=512, ab_out=512, mem=512, moe=1024, mm=1024, gdn_prep=256)
    y_p, k_p, v_p, sconv_p, gdn_s_p, gdn_conv_p = _trunk(
        x_prompt.reshape(seq, D_MODEL), b, jnp.arange(seq, dtype=jnp.int32), moba_p,
        jnp.zeros((b, SCONV_W - 1, SCONV_CH), F32),
        jnp.zeros((b, G_V_HEADS, G_DK, G_DV), F32),
        jnp.zeros((b, G_CONV_W - 1, G_QKV_DIM), F32),
        mem_k_p.reshape(DEPTH * b, MEM_LEN, mw), mem_v_p.reshape(DEPTH * b, MEM_LEN, mw), wts, tiles_p)

    past_len = page_table.shape[1] * PAGE
    pos_s = jnp.tile(past_len + jnp.arange(dseq, dtype=jnp.int32), db)

    def moba_s(qb, kf, kb, vb):
        return _moba_sample(qb, kb, vb, cache_moba_k, cache_moba_v, page_table, dseq)

    def mem_stored(c):
        chunks = MEM_HD // 128
        c = c.reshape(DEPTH * db, MEM_LEN, MEM_HEADS, chunks, 128).transpose(0, 1, 3, 2, 4)
        return c.reshape(DEPTH * db, MEM_LEN * chunks * MEM_HEADS, 128)

    ns = db * dseq
    tiles_s = dict(ab_in=ns, ab_out=dseq, mem=dseq, moe=ns, mm=ns, gdn_prep=dseq)
    y_s, k_s, v_s, sconv_s, gdn_s_s, gdn_conv_s = _trunk(
        x_sample.reshape(ns, D_MODEL), db, pos_s, moba_s, state_sconv, state_gdn, state_gdn_conv,
        mem_stored(cache_mem_k), mem_stored(cache_mem_v), wts, tiles_s)

    hd = (HEADS_A, HD_A)
    return (y_p.reshape(b, seq, D_MODEL), y_s.reshape(db, dseq, D_MODEL),
            k_p.reshape(b, seq, *hd), v_p.reshape(b, seq, *hd),
            sconv_p, gdn_s_p, gdn_conv_p,
            mem_k_p.reshape(DEPTH, b, MEM_LEN, MEM_HEADS, MEM_HD),
            mem_v_p.reshape(DEPTH, b, MEM_LEN, MEM_HEADS, MEM_HD),
            k_s.reshape(db, dseq, *hd), v_s.reshape(db, dseq, *hd),
            sconv_s, gdn_s_s, gdn_conv_s)
```

```python
import functools
import math

import jax
import jax.numpy as jnp
from jax import lax
from jax.experimental import pallas as pl
from jax.experimental.pallas import tpu as pltpu

F32 = jnp.float32
BF16 = jnp.bfloat16

D_MODEL = 1024
DEPTH = 2
PAGE = 128
HEADS_A = 8
HD_A = 64
AW = HEADS_A * HD_A
KBLK = 256
TOPK_A = 3
ROPE_THETA = 500000.0
ROPE_DIMS = HD_A // 4
SCONV_CH = 512
SCONV_W = 3
G_QK_HEADS = 8
G_V_HEADS = 16
G_DK = 128
G_DV = 128
G_CONV_W = 4
G_CHUNK = 64
G_QK_DIM = G_QK_HEADS * G_DK
G_V_DIM = G_V_HEADS * G_DV
G_QKV_DIM = 2 * G_QK_DIM + G_V_DIM
MEM_LEN = 256
MEM_HEADS = 4
MEM_HD = D_MODEL // MEM_HEADS
N_EXPERTS = 16
N_GROUPS = 4
EXP_PER_GROUP = N_EXPERTS // N_GROUPS
EXPERT_FF = 512
DN_ALPHA = (2 * DEPTH) ** 0.25
LN_EPS = 1e-5
RMS_EPS = 1e-6
NEG_BIG = -1e30

SUBLANES = 8
MXU_DIM = 256
SAMPLE_PAGES_PER_STEP = 16
MIB = 1024 * 1024

_NT = (((1,), (1,)), ((), ()))
_TN = (((0,), (0,)), ((), ()))


def _cp(sem, vmem_mib):
    return pltpu.CompilerParams(dimension_semantics=sem, vmem_limit_bytes=vmem_mib * MIB)


def _dot(a, b):
    return jnp.dot(a, b, preferred_element_type=F32)


def _dot_nt(a, b):
    return lax.dot_general(a, b, _NT, preferred_element_type=F32)


def _split3(a):
    a0 = a.astype(BF16)
    r = a - a0.astype(F32)
    a1 = r.astype(BF16)
    a2 = (r - a1.astype(F32)).astype(BF16)
    return a0, a1, a2


def _dot_bf(a, b):
    return _dot(a.astype(BF16), b.astype(BF16))


def _layernorm(y, g, b):
    mu = jnp.mean(y, axis=-1, keepdims=True)
    yc = y - mu
    var = jnp.mean(yc * yc, axis=-1, keepdims=True)
    return yc * lax.rsqrt(var + LN_EPS) * g + b


def _sigmoid(x):
    return 1.0 / (1.0 + jnp.exp(-x))


def _silu(x):
    return x * _sigmoid(x)


def _mm_kernel(x_ref, w_ref, o_ref, xb_ref):
    @pl.when(pl.program_id(1) == 0)
    def _():
        xb_ref[...] = x_ref[...].astype(BF16)

    o_ref[...] = _dot(xb_ref[...], w_ref[...])


def _mm(x, w, tm, tn, name):
    n, k = x.shape
    f = w.shape[1]
    return pl.pallas_call(
        _mm_kernel,
        grid=(n // tm, f // tn),
        in_specs=[pl.BlockSpec((tm, k), lambda i, j: (i, 0)),
                  pl.BlockSpec((k, tn), lambda i, j: (0, j))],
        out_specs=pl.BlockSpec((tm, tn), lambda i, j: (i, j)),
        out_shape=jax.ShapeDtypeStruct((n, f), F32),
        scratch_shapes=[pltpu.VMEM((tm, k), BF16)],
        compiler_params=_cp(("arbitrary", "arbitrary"), 48),
        name=name,
    )(x, w)


def _mm_res_ln_kernel(a_ref, w_ref, x_ref, g_ref, b_ref, o_ref):
    h = _dot(a_ref[...].astype(BF16), w_ref[...])
    o_ref[...] = _layernorm(DN_ALPHA * x_ref[...] + h, g_ref[...], b_ref[...])


def _mm_res_ln(a, w, x, g, b, tm, name):
    n, k = a.shape
    return pl.pallas_call(
        _mm_res_ln_kernel,
        grid=(n // tm,),
        in_specs=[pl.BlockSpec((tm, k), lambda i: (i, 0)),
                  pl.BlockSpec((k, D_MODEL), lambda i: (0, 0)),
                  pl.BlockSpec((tm, D_MODEL), lambda i: (i, 0)),
                  pl.BlockSpec((1, D_MODEL), lambda i: (0, 0)),
                  pl.BlockSpec((1, D_MODEL), lambda i: (0, 0))],
        out_specs=pl.BlockSpec((tm, D_MODEL), lambda i: (i, 0)),
        out_shape=jax.ShapeDtypeStruct((n, D_MODEL), F32),
        compiler_params=_cp(("arbitrary",), 48),
        name=name,
    )(a, w, x, g, b)


def _ab_in_kernel(x_ref, w_ref, c_ref, sa_ref, sb_ref,
                  qb_ref, kf_ref, kb_ref, vf_ref, vb_ref, cu_ref, gb_ref):
    y = _dot(x_ref[...].astype(BF16), w_ref[...])

    def tile4(t):
        t = t[...]
        return jnp.concatenate([t, t, t, t], axis=1)

    c, sa, sb = tile4(c_ref), tile4(sa_ref), tile4(sb_ref)

    def rope(t):
        return (t * c + pltpu.roll(t, AW - ROPE_DIMS // 2, 1) * sa
                + pltpu.roll(t, ROPE_DIMS // 2, 1) * sb)

    q = rope(y[:, 0:AW])
    k = rope(y[:, AW:2 * AW])
    v = y[:, 2 * AW:3 * AW]
    u = y[:, 3 * AW:3 * AW + SCONV_CH]
    gate_b = y[:, 3 * AW + SCONV_CH:3 * AW + 2 * SCONV_CH]
    gate_c = y[:, 3 * AW + 2 * SCONV_CH:]
    qb_ref[...] = (q * (HD_A ** -0.5)).astype(BF16)
    kf_ref[...] = k
    kb_ref[...] = k.astype(BF16)
    vf_ref[...] = v
    vb_ref[...] = v.astype(BF16)
    cu_ref[...] = gate_c * u
    gb_ref[...] = gate_b


def _ab_in(x, w, tabs, tm):
    n = x.shape[0]
    fw = w.shape[1]
    row = lambda i: (i, 0)
    half = pl.BlockSpec((tm, AW), row)
    tab = pl.BlockSpec((tm, 128), row)
    return pl.pallas_call(
        _ab_in_kernel,
        grid=(n // tm,),
        in_specs=[pl.BlockSpec((tm, D_MODEL), row),
                  pl.BlockSpec((D_MODEL, fw), lambda i: (0, 0)),
                  tab, tab, tab],
        out_specs=[half] * 7,
        out_shape=[jax.ShapeDtypeStruct((n, AW), BF16),
                   jax.ShapeDtypeStruct((n, AW), F32),
                   jax.ShapeDtypeStruct((n, AW), BF16),
                   jax.ShapeDtypeStruct((n, AW), F32),
                   jax.ShapeDtypeStruct((n, AW), BF16),
                   jax.ShapeDtypeStruct((n, AW), F32),
                   jax.ShapeDtypeStruct((n, AW), F32)],
        compiler_params=_cp(("arbitrary",), 48),
        name="ab_in",
    )(x, w, *tabs)


def _rope_tables(pos):
    half = ROPE_DIMS // 2
    inv_freq = ROPE_THETA ** (-2.0 * jnp.arange(half, dtype=F32) / ROPE_DIMS)
    ang = pos.astype(F32)[:, None] * inv_freq[None, :]
    cos, sin = jnp.cos(ang), jnp.sin(ang)
    n = pos.shape[0]
    rest1 = jnp.ones((n, HD_A - ROPE_DIMS), F32)
    rest0 = jnp.zeros((n, HD_A - ROPE_DIMS), F32)
    z = jnp.zeros((n, half), F32)
    c = jnp.concatenate([cos, cos, rest1], axis=1)
    sa = jnp.concatenate([-sin, z, rest0], axis=1)
    sb = jnp.concatenate([z, sin, rest0], axis=1)
    return tuple(jnp.concatenate([t, t], axis=1) for t in (c, sa, sb))


def _block_mean_kernel(k_ref, o_ref):
    o_ref[0] = jnp.sum(k_ref[...], axis=0, keepdims=True) * (1.0 / KBLK)


def _block_mean(k):
    n = k.shape[0]
    nb = n // KBLK
    out = pl.pallas_call(
        _block_mean_kernel,
        grid=(nb,),
        in_specs=[pl.BlockSpec((KBLK, AW), lambda i: (i, 0))],
        out_specs=pl.BlockSpec((1, 1, AW), lambda i: (i, 0, 0)),
        out_shape=jax.ShapeDtypeStruct((nb, 1, AW), F32),
        compiler_params=_cp(("arbitrary",), 32),
        name="moba_block_mean",
    )(k)
    return out.reshape(nb, AW)


def _top3_mask(g, idx, n_idx, axis):
    sel = jnp.zeros(g.shape, jnp.bool_)
    for _ in range(TOPK_A):
        mx = jnp.max(g, axis=axis, keepdims=True)
        first = jnp.min(jnp.where(g == mx, idx, n_idx), axis=axis, keepdims=True)
        pick = idx == first
        sel = jnp.logical_or(sel, pick)
        g = jnp.where(pick, -jnp.inf, g)
    return sel


def _moba_prompt_kernel(q_ref, k_ref, vt2_ref, vto_ref, km_ref, o_ref, bias_ref, s0_ref, s1_ref):
    i = pl.program_id(1)
    nb = km_ref.shape[0]
    q = q_ref[...]
    km = km_ref[...].astype(BF16)
    lane = lax.broadcasted_iota(jnp.int32, q.shape, 1)
    blk = lax.broadcasted_iota(jnp.int32, (nb, KBLK), 0)
    kpos = lax.broadcasted_iota(jnp.int32, (KBLK, KBLK), 0)
    qpos = lax.broadcasted_iota(jnp.int32, (KBLK, KBLK), 1)
    k_own = k_ref[pl.ds(pl.multiple_of(i * KBLK, KBLK), KBLK), :]
    qhs, init = [], []
    for h in range(2):
        qh = jnp.where(lane // HD_A == h, q, jnp.zeros_like(q))
        qhs.append(qh)
        gate = _dot_nt(km, qh)
        gate = jnp.where(blk < i, gate, -jnp.inf)
        sel = jnp.logical_and(_top3_mask(gate, blk, nb, 0), blk < i)
        bias_ref[h, 0:nb, :] = jnp.where(sel, 0.0, NEG_BIG)
        bias_ref[h, nb:nb + SUBLANES, :] = jnp.full((SUBLANES, KBLK), NEG_BIG, F32)
        s = _dot_nt(k_own, qh)
        s = jnp.where(kpos <= qpos, s, NEG_BIG)
        m = jnp.max(s, axis=0, keepdims=True)
        p = jnp.exp(s - m)
        l = jnp.sum(p, axis=0, keepdims=True)
        acc = _dot(vto_ref[0, h * HD_A:(h + 1) * HD_A, :], p.astype(BF16))
        init += [m, l, acc]

    last_pair = nb // 2 - 1

    def qk_into(dst_ref, pair):
        pc = jnp.minimum(pair, last_pair)
        k2 = k_ref[pl.ds(pl.multiple_of(pc * 2 * KBLK, 2 * KBLK), 2 * KBLK), :]
        for h in range(2):
            dst_ref[h] = _dot_nt(k2, qhs[h])

    def consume(src_ref, pair, carry):
        pc = jnp.minimum(pair, last_pair)
        out = []
        for h in range(2):
            m, l, acc = carry[3 * h:3 * h + 3]
            sa = src_ref[h, 0:KBLK, :] + bias_ref[h, pl.ds(2 * pair, 1), :]
            sb = src_ref[h, KBLK:2 * KBLK, :] + bias_ref[h, pl.ds(2 * pair + 1, 1), :]
            m_new = jnp.maximum(m, jnp.maximum(jnp.max(sa, axis=0, keepdims=True),
                                               jnp.max(sb, axis=0, keepdims=True)))
            a = jnp.exp(m - m_new)
            pa = jnp.exp(sa - m_new)
            pb = jnp.exp(sb - m_new)
            l = a * l + jnp.sum(pa, axis=0, keepdims=True) + jnp.sum(pb, axis=0, keepdims=True)
            p2 = jnp.concatenate([pa, pb], axis=0).astype(BF16)
            acc = a * acc + _dot(vt2_ref[pc, h * HD_A:(h + 1) * HD_A, :], p2)
            out += [m_new, l, acc]
        return tuple(out)

    def body(t, carry):
        qk_into(s1_ref, 2 * t + 1)
        carry = consume(s0_ref, 2 * t, carry)
        qk_into(s0_ref, 2 * t + 2)
        return consume(s1_ref, 2 * t + 1, carry)

    npairs = (i + 1) // 2
    qk_into(s0_ref, 0)
    res = lax.fori_loop(0, (npairs + 1) // 2, body, tuple(init))
    outs = [res[2] / res[1], res[5] / res[4]]
    o_ref[...] = jnp.concatenate(outs, axis=0).T


def _moba_prompt(qb, kb, vb, km):
    n = qb.shape[0]
    nb = n // KBLK
    assert nb % 2 == 0
    pairs = AW // 128
    vt_own = vb.reshape(nb, KBLK, AW).transpose(0, 2, 1)
    vt_two = vb.reshape(nb // 2, 2 * KBLK, AW).transpose(0, 2, 1)
    return pl.pallas_call(
        _moba_prompt_kernel,
        grid=(pairs, nb),
        in_specs=[pl.BlockSpec((KBLK, 128), lambda p, i: (i, p)),
                  pl.BlockSpec((n, 128), lambda p, i: (0, p)),
                  pl.BlockSpec((nb // 2, 128, 2 * KBLK), lambda p, i: (0, p, 0)),
                  pl.BlockSpec((1, 128, KBLK), lambda p, i: (i, p, 0)),
                  pl.BlockSpec((nb, 128), lambda p, i: (0, p))],
        out_specs=pl.BlockSpec((KBLK, 128), lambda p, i: (i, p)),
        out_shape=jax.ShapeDtypeStruct((n, AW), F32),
        scratch_shapes=[pltpu.VMEM((2, nb + SUBLANES, KBLK), F32),
                        pltpu.VMEM((2, 2 * KBLK, KBLK), F32),
                        pltpu.VMEM((2, 2 * KBLK, KBLK), F32)],
        compiler_params=_cp(("arbitrary", "arbitrary"), 48),
        name="moba_prompt",
    )(qb, kb, vt_two, vt_own, km)


def _moba_sample_kernel(pt_ref, q_ref, kn_ref, vn_ref, *rest, nblk, tq):
    del pt_ref
    npg = SAMPLE_PAGES_PER_STEP
    k_refs, v_refs = rest[:npg], rest[npg:2 * npg]
    o_ref, qexp_ref, gate_ref, s_ref, acc_ref, l_ref = rest[2 * npg:]
    j = pl.program_id(1)
    nsteps = 2 * nblk // npg
    bps = npg // 2

    @pl.when(j == 0)
    def _():
        q = q_ref[...]
        lane = lax.broadcasted_iota(jnp.int32, q.shape, 1)
        qexp_ref[...] = jnp.concatenate(
            [jnp.where(lane // HD_A == h, q, jnp.zeros_like(q)) for h in range(HEADS_A)], axis=0)
        gate_ref[...] = jnp.full(gate_ref.shape, -jnp.inf, F32)

    @pl.when(j < nsteps)
    def _():
        qe = qexp_ref[...]
        bl = lax.broadcasted_iota(jnp.int32, gate_ref.shape, 1)
        pages = [r[0] for r in k_refs]
        kms = [jnp.sum(pages[2 * bi] + pages[2 * bi + 1], axis=1, keepdims=True) * (1.0 / KBLK)
               for bi in range(bps)]
        rhs = jnp.concatenate([p.astype(BF16) for p in pages]
                              + [jnp.broadcast_to(km, (AW, 128)).astype(BF16) for km in kms], axis=1)
        s_all = _dot(qe, rhs)
        gate = gate_ref[...]
        for bi in range(bps):
            blk = j * bps + bi
            s_ref[blk] = s_all[:, bi * KBLK:(bi + 1) * KBLK]
            g = s_all[:, (npg + bi) * 128:(npg + bi + 1) * 128]
            gate = jnp.where(bl == blk, g, gate)
        gate_ref[...] = gate

    @pl.when(j == nsteps - 1)
    def _():
        qe = qexp_ref[...]
        gate = gate_ref[...]
        bl = lax.broadcasted_iota(jnp.int32, gate.shape, 1)
        self_f = jnp.logical_and(_top3_mask(gate, bl, gate.shape[1], 1), bl < nblk).astype(F32)

        s_own = _dot_nt(qe, kn_ref[0])
        rq = lax.broadcasted_iota(jnp.int32, s_own.shape, 0) % tq
        kt = lax.broadcasted_iota(jnp.int32, s_own.shape, 1)
        s_own = jnp.where(kt <= rq, s_own, NEG_BIG)
        m = jnp.max(s_own, axis=1, keepdims=True)
        picked = [self_f[:, jj:jj + 1] > 0.0 for jj in range(nblk)]
        for jj in range(nblk):
            sj = jnp.where(picked[jj], s_ref[jj], NEG_BIG)
            m = jnp.maximum(m, jnp.max(sj, axis=1, keepdims=True))
        p_own = jnp.exp(s_own - m)
        l = jnp.sum(p_own, axis=1, keepdims=True)
        for jj in range(nblk):
            pj = jnp.where(picked[jj], jnp.exp(s_ref[jj] - m), 0.0)
            s_ref[jj] = pj
            l = l + jnp.sum(pj, axis=1, keepdims=True)
        l_ref[...] = jnp.broadcast_to(l, l_ref.shape)
        acc_ref[...] = _dot(p_own.astype(BF16), vn_ref[0])

    @pl.when(j >= nsteps)
    def _():
        p = jnp.concatenate([s_ref[(j - nsteps) * bps + bi] for bi in range(bps)], axis=1).astype(BF16)
        vt = jnp.concatenate([r[0].astype(BF16) for r in v_refs], axis=1)
        acc_ref[...] += _dot_nt(p, vt)

    @pl.when(j == 2 * nsteps - 1)
    def _():
        o = acc_ref[...] / l_ref[:, 0:1]
        lane = lax.broadcasted_iota(jnp.int32, (tq, AW), 1)
        res = jnp.zeros((tq, AW), F32)
        for h in range(HEADS_A):
            res = res + jnp.where(lane // HD_A == h, o[h * tq:(h + 1) * tq, :], 0.0)
        o_ref[...] = res


def _moba_sample(qb, kb_new, vb_new, cache_k, cache_v, page_table, tq):
    n = qb.shape[0]
    nseq = n // tq
    npages = page_table.shape[1]
    nblk = npages * PAGE // KBLK
    npg = SAMPLE_PAGES_PER_STEP
    assert KBLK == 2 * PAGE and tq <= 128 and nblk <= 128 and npages % npg == 0
    nsteps = npages // npg
    ck = cache_k.transpose(0, 2, 3, 1).reshape(cache_k.shape[0], AW, PAGE)
    cv = cache_v.transpose(0, 2, 3, 1).reshape(cache_v.shape[0], AW, PAGE)
    pad = ((0, 0), (0, 128 - tq), (0, 0))
    kn = jnp.pad(kb_new.reshape(nseq, tq, AW), pad)
    vn = jnp.pad(vb_new.reshape(nseq, tq, AW), pad)
    rows = HEADS_A * tq

    def kpage(off):
        return pl.BlockSpec((1, AW, PAGE),
                            lambda b, j, pt: (pt[b, npg * jnp.minimum(j, nsteps - 1) + off], 0, 0))

    def vpage(off):
        return pl.BlockSpec((1, AW, PAGE),
                            lambda b, j, pt: (pt[b, npg * jnp.maximum(j - nsteps, 0) + off], 0, 0))

    seq3 = pl.BlockSpec((1, 128, AW), lambda b, j, pt: (b, 0, 0))
    grid_spec = pltpu.PrefetchScalarGridSpec(
        num_scalar_prefetch=1,
        grid=(nseq, 2 * nsteps),
        in_specs=([pl.BlockSpec((tq, AW), lambda b, j, pt: (b, 0)), seq3, seq3]
                  + [kpage(off) for off in range(npg)] + [vpage(off) for off in range(npg)]),
        out_specs=pl.BlockSpec((tq, AW), lambda b, j, pt: (b, 0)),
        scratch_shapes=[pltpu.VMEM((rows, AW), BF16),
                        pltpu.VMEM((rows, 128), F32),
                        pltpu.VMEM((nblk, rows, KBLK), F32),
                        pltpu.VMEM((rows, AW), F32),
                        pltpu.VMEM((rows, 128), F32)])
    return pl.pallas_call(
        functools.partial(_moba_sample_kernel, nblk=nblk, tq=tq),
        grid_spec=grid_spec,
        out_shape=jax.ShapeDtypeStruct((n, AW), F32),
        compiler_params=_cp(("arbitrary", "arbitrary"), 32),
        name="moba_sample",
    )(page_table, qb, kn, vn, *([ck] * npg), *([cv] * npg))


def _ab_out_kernel(attn_ref, cu_ref, cup_ref, hist_ref, gb_ref, x_ref, w_ref, cw_ref, g_ref, b_ref,
                   o_ref, hbuf_ref, *, tt):
    j = pl.program_id(1)
    hbuf_ref[0:SUBLANES, :] = jnp.where(j == 0, hist_ref[0], cup_ref[...])
    hbuf_ref[SUBLANES:, :] = cu_ref[...]
    cw = cw_ref[...]
    conv = jnp.zeros((tt, SCONV_CH), F32)
    for tap in range(SCONV_W):
        r0 = SUBLANES - (SCONV_W - 1) + tap
        conv = conv + hbuf_ref[r0:r0 + tt, :] * cw[tap:tap + 1, :]
    conv = gb_ref[...] * conv
    mix = jnp.concatenate([attn_ref[...], conv], axis=1).astype(BF16)
    h = _dot(mix, w_ref[...])
    o_ref[...] = _layernorm(DN_ALPHA * x_ref[...] + h, g_ref[...], b_ref[...])


def _ab_out(attn, cu, hist, gb, x, w, cw, g, b, nseq, tt):
    n = x.shape[0]
    t = n // nseq
    nt = t // tt
    hist8 = jnp.pad(hist, ((0, 0), (SUBLANES - hist.shape[1], 0), (0, 0)))
    row = lambda s, j: (s * nt + j, 0)
    prev = lambda s, j: (jnp.maximum((s * t + j * tt) // SUBLANES - 1, 0), 0)
    const = lambda s, j: (0, 0)
    return pl.pallas_call(
        functools.partial(_ab_out_kernel, tt=tt),
        grid=(nseq, nt),
        in_specs=[pl.BlockSpec((tt, AW), row),
                  pl.BlockSpec((tt, SCONV_CH), row),
                  pl.BlockSpec((SUBLANES, SCONV_CH), prev),
                  pl.BlockSpec((1, SUBLANES, SCONV_CH), lambda s, j: (s, 0, 0)),
                  pl.BlockSpec((tt, SCONV_CH), row),
                  pl.BlockSpec((tt, D_MODEL), row),
                  pl.BlockSpec((AW + SCONV_CH, D_MODEL), const),
                  pl.BlockSpec((SCONV_W, SCONV_CH), const),
                  pl.BlockSpec((1, D_MODEL), const),
                  pl.BlockSpec((1, D_MODEL), const)],
        out_specs=pl.BlockSpec((tt, D_MODEL), row),
        out_shape=jax.ShapeDtypeStruct((n, D_MODEL), F32),
        scratch_shapes=[pltpu.VMEM((tt + SUBLANES, SCONV_CH), F32)],
        compiler_params=_cp(("arbitrary", "arbitrary"), 48),
        name="ab_out",
    )(attn, cu, cu, hist8, gb, x, w, cw, g, b)


def _mem_head(ref, h):
    if ref.shape[1] == MEM_LEN:
        return ref[0, :, h * MEM_HD:(h + 1) * MEM_HD]
    chunks = MEM_HD // 128
    return jnp.concatenate([ref[0, pl.ds(c * MEM_HEADS + h, MEM_LEN, stride=chunks * MEM_HEADS), :]
                            for c in range(chunks)], axis=1)


def _memattn_kernel(x_ref, wq_ref, mk_ref, mv_ref, wo_ref, g_ref, b_ref, o_ref):
    x = x_ref[...]
    q = _dot(x.astype(BF16), wq_ref[...])
    outs = []
    for h in range(MEM_HEADS):
        sl = slice(h * MEM_HD, (h + 1) * MEM_HD)
        s = _dot_nt(q[:, sl].astype(BF16), _mem_head(mk_ref, h).astype(BF16)) * (MEM_HD ** -0.5)
        s = s - jnp.max(s, axis=-1, keepdims=True)
        p = jnp.exp(s)
        p = p / jnp.sum(p, axis=-1, keepdims=True)
        outs.append(_dot(p.astype(BF16), _mem_head(mv_ref, h).astype(BF16)))
    o = jnp.concatenate(outs, axis=1).astype(BF16)
    y = _dot(o, wo_ref[...])
    o_ref[...] = _layernorm(DN_ALPHA * x + y, g_ref[...], b_ref[...])


def _memattn(x, wq, mk, mv, wo, g, b, tm, mem_base, nb):
    n = x.shape[0]
    per = n // tm // nb
    row = lambda i: (i, 0)
    const = lambda i: (0, 0)
    mem = pl.BlockSpec((1,) + mk.shape[1:], lambda i: (mem_base + i // per, 0, 0))
    return pl.pallas_call(
        _memattn_kernel,
        grid=(n // tm,),
        in_specs=[pl.BlockSpec((tm, D_MODEL), row),
                  pl.BlockSpec((D_MODEL, D_MODEL), const),
                  mem, mem,
                  pl.BlockSpec((D_MODEL, D_MODEL), const),
                  pl.BlockSpec((1, D_MODEL), const),
                  pl.BlockSpec((1, D_MODEL), const)],
        out_specs=pl.BlockSpec((tm, D_MODEL), row),
        out_shape=jax.ShapeDtypeStruct((n, D_MODEL), F32),
        compiler_params=_cp(("arbitrary",), 48),
        name="memattn",
    )(x, wq, mk, mv, wo, g, b)


def _route(logits, bias):
    lane = lax.broadcasted_iota(jnp.int32, logits.shape, 1)
    grp = lane // EXP_PER_GROUP
    ex = jnp.exp(logits - jnp.max(logits, axis=1, keepdims=True))
    scores = ex / jnp.sum(ex, axis=1, keepdims=True)
    biased = scores + bias

    def top2(vals):
        t1 = jnp.max(vals, axis=1, keepdims=True)
        i1 = jnp.min(jnp.where(vals == t1, lane, N_EXPERTS), axis=1, keepdims=True)
        rest = jnp.where(lane == i1, -jnp.inf, vals)
        t2 = jnp.max(rest, axis=1, keepdims=True)
        i2 = jnp.min(jnp.where(rest == t2, lane, N_EXPERTS), axis=1, keepdims=True)
        return t1, i1, t2, i2

    best = jnp.zeros((logits.shape[0], 1), jnp.int32)
    best_v = None
    for gi in range(N_GROUPS):
        t1, _, t2, _ = top2(jnp.where(grp == gi, biased, -jnp.inf))
        gs = t1 + t2
        if best_v is None:
            best_v = gs
        else:
            better = gs > best_v
            best = jnp.where(better, gi, best)
            best_v = jnp.where(better, gs, best_v)
    _, i1, _, i2 = top2(jnp.where(grp == best, biased, -jnp.inf))
    s1 = jnp.sum(jnp.where(lane == i1, scores, 0.0), axis=1, keepdims=True)
    s2 = jnp.sum(jnp.where(lane == i2, scores, 0.0), axis=1, keepdims=True)
    den = s1 + s2
    return jnp.where(lane == i1, s1 / den, 0.0) + jnp.where(lane == i2, s2 / den, 0.0)


def _moe_kernel(x_ref, wr_ref, rb_ref, wg_ref, wu_ref, wd_ref, g_ref, b_ref, o_ref,
                xb_ref, comb_ref, acc_ref):
    e = pl.program_id(1)

    @pl.when(e == 0)
    def _():
        x = x_ref[...]
        xb_ref[...] = x.astype(BF16)
        comb_ref[...] = _route(_dot_bf(x, wr_ref[...]), rb_ref[...])
        acc_ref[...] = jnp.zeros_like(acc_ref)

    xb = xb_ref[...]
    hg = _dot(xb, wg_ref[0].astype(BF16))
    hu = _dot(xb, wu_ref[0].astype(BF16))
    comb = comb_ref[...]
    lane = lax.broadcasted_iota(jnp.int32, comb.shape, 1)
    ce = jnp.sum(jnp.where(lane == e, comb, 0.0), axis=1, keepdims=True)
    h = _silu(hg) * hu * ce
    acc_ref[...] += _dot(h.astype(BF16), wd_ref[0].astype(BF16))

    @pl.when(e == N_EXPERTS - 1)
    def _():
        o_ref[...] = _layernorm(DN_ALPHA * x_ref[...] + acc_ref[...], g_ref[...], b_ref[...])


def _moe(x, wr, rb, wg, wu, wd, g, b, tm, layer):
    n = x.shape[0]
    row = lambda i, e: (i, 0)
    const = lambda i, e: (0, 0)
    expert = lambda i, e: (layer, e, 0, 0)
    return pl.pallas_call(
        _moe_kernel,
        grid=(n // tm, N_EXPERTS),
        in_specs=[pl.BlockSpec((tm, D_MODEL), row),
                  pl.BlockSpec((D_MODEL, N_EXPERTS), const),
                  pl.BlockSpec((1, N_EXPERTS), const),
                  pl.BlockSpec((None, 1, D_MODEL, EXPERT_FF), expert),
                  pl.BlockSpec((None, 1, D_MODEL, EXPERT_FF), expert),
                  pl.BlockSpec((None, 1, EXPERT_FF, D_MODEL), expert),
                  pl.BlockSpec((1, D_MODEL), const),
                  pl.BlockSpec((1, D_MODEL), const)],
        out_specs=pl.BlockSpec((tm, D_MODEL), row),
        out_shape=jax.ShapeDtypeStruct((n, D_MODEL), F32),
        scratch_shapes=[pltpu.VMEM((tm, D_MODEL), BF16),
                        pltpu.VMEM((tm, N_EXPERTS), F32),
                        pltpu.VMEM((tm, D_MODEL), F32)],
        compiler_params=_cp(("arbitrary", "arbitrary"), 56),
        name="moe",
    )(x, wr, rb, wg, wu, wd, g, b)


def _gdn_prep_kernel(qkv_ref, prev_ref, hist_ref, cw_ref, x_ref, wa_ref, wb_ref, alog_ref, dtb_ref,
                     qn_ref, kn_ref, v_ref, g_ref, beta_ref, hbuf_ref, *, tt):
    j = pl.program_id(1)
    hbuf_ref[0:SUBLANES, :] = jnp.where(j == 0, hist_ref[0], prev_ref[...])
    hbuf_ref[SUBLANES:, :] = qkv_ref[...]
    for c in range(G_QKV_DIM // 128):
        sl = slice(c * 128, (c + 1) * 128)
        y = jnp.zeros((tt, 128), F32)
        for tap in range(G_CONV_W):
            r0 = SUBLANES - (G_CONV_W - 1) + tap
            y = y + hbuf_ref[r0:r0 + tt, sl] * cw_ref[tap:tap + 1, sl]
        y = _silu(y)
        if c < 2 * G_QK_HEADS:
            y = y * lax.rsqrt(jnp.sum(y * y, axis=-1, keepdims=True) + RMS_EPS)
            if c < G_QK_HEADS:
                qn_ref[:, sl] = y * (G_DK ** -0.5)
            else:
                kn_ref[:, (c - G_QK_HEADS) * 128:(c - G_QK_HEADS + 1) * 128] = y
        else:
            v_ref[:, (c - 2 * G_QK_HEADS) * 128:(c - 2 * G_QK_HEADS + 1) * 128] = y
    x = x_ref[...]
    a = _dot_bf(x, wa_ref[...]) + dtb_ref[...]
    softplus = jnp.maximum(a, 0.0) + jnp.log(1.0 + jnp.exp(-jnp.abs(a)))
    g_ref[...] = -jnp.exp(alog_ref[...]) * softplus
    beta_ref[...] = _sigmoid(_dot_bf(x, wb_ref[...]))


def _gdn_prep(big, hist, cw, x, wa, wb, alog, dtb, nseq, tt):
    n = x.shape[0]
    t = n // nseq
    nt = t // tt
    hist8 = jnp.pad(hist, ((0, 0), (SUBLANES - hist.shape[1], 0), (0, 0)))
    row = lambda s, j: (s * nt + j, 0)
    prev = lambda s, j: (jnp.maximum((s * t + j * tt) // SUBLANES - 1, 0), 0)
    const = lambda s, j: (0, 0)
    hv = G_V_HEADS
    return pl.pallas_call(
        functools.partial(_gdn_prep_kernel, tt=tt),
        grid=(nseq, nt),
        in_specs=[pl.BlockSpec((tt, G_QKV_DIM), row),
                  pl.BlockSpec((SUBLANES, G_QKV_DIM), prev),
                  pl.BlockSpec((1, SUBLANES, G_QKV_DIM), lambda s, j: (s, 0, 0)),
                  pl.BlockSpec((G_CONV_W, G_QKV_DIM), const),
                  pl.BlockSpec((tt, D_MODEL), row),
                  pl.BlockSpec((D_MODEL, hv), const),
                  pl.BlockSpec((D_MODEL, hv), const),
                  pl.BlockSpec((1, hv), const),
                  pl.BlockSpec((1, hv), const)],
        out_specs=[pl.BlockSpec((tt, G_QK_DIM), row),
                   pl.BlockSpec((tt, G_QK_DIM), row),
                   pl.BlockSpec((tt, G_V_DIM), row),
                   pl.BlockSpec((tt, hv), row),
                   pl.BlockSpec((tt, hv), row)],
        out_shape=[jax.ShapeDtypeStruct((n, G_QK_DIM), F32),
                   jax.ShapeDtypeStruct((n, G_QK_DIM), F32),
                   jax.ShapeDtypeStruct((n, G_V_DIM), F32),
                   jax.ShapeDtypeStruct((n, hv), F32),
                   jax.ShapeDtypeStruct((n, hv), F32)],
        scratch_shapes=[pltpu.VMEM((tt + SUBLANES, G_QKV_DIM), F32)],
        compiler_params=_cp(("arbitrary", "arbitrary"), 48),
        name="gdn_prep",
    )(big, big, hist8, cw, x, wa, wb, alog, dtb)


def _gdn_geometry(c):
    hpg = min(G_V_HEADS, MXU_DIM // c)
    hpt = 128 // c
    return hpg, G_V_HEADS // hpg, hpt, G_V_HEADS // hpt


def _gdn_local_kernel(q_ref, k_ref, v_ref, g_ref, beta_ref,
                      u_ref, wq_ref, qk_ref, kdt_ref, egl_ref, *, c):
    hpg, ngroups, _, _ = _gdn_geometry(c)
    r = hpg * c
    rep = G_V_HEADS // G_QK_HEADS
    ri = lax.broadcasted_iota(jnp.int32, (c, c), 0)
    ci = lax.broadcasted_iota(jnp.int32, (c, c), 1)
    tri = (ri >= ci).astype(BF16)
    g0, g1, g2 = _split3(g_ref[...])
    gcum = _dot(tri, g0) + _dot(tri, g1) + _dot(tri, g2)
    beta = beta_ref[...]
    egl_ref[0] = jnp.exp(gcum[c - 1:c, :])

    rr = lax.broadcasted_iota(jnp.int32, (r, r), 0)
    cr = lax.broadcasted_iota(jnp.int32, (r, r), 1)
    same = (rr // c) == (cr // c)
    incl = jnp.logical_and(same, rr >= cr)
    strict = jnp.logical_and(same, rr > cr)
    eye = (rr == cr).astype(F32)

    def stack(fn, heads):
        return jnp.concatenate([fn(h) for h in heads], axis=0)

    groups = [list(range(gi * hpg, (gi + 1) * hpg)) for gi in range(ngroups)]
    k_st, q_st, kb_st, gc_col, decay, x, tinv = [], [], [], [], [], [], []
    for heads in groups:
        ks = stack(lambda h: k_ref[:, (h // rep) * G_DK:(h // rep + 1) * G_DK], heads)
        bt = stack(lambda h: beta[:, h:h + 1], heads)
        gc = stack(lambda h: gcum[:, h:h + 1], heads)
        gmat = jnp.broadcast_to(gc, (r, r))
        dec = jnp.exp(jnp.where(incl, gmat - gmat.T, -jnp.inf))
        kb = ks * bt
        xg = -jnp.where(strict, _dot_nt(kb.astype(BF16), ks.astype(BF16)) * dec, 0.0)
        k_st.append(ks); kb_st.append(kb); gc_col.append(gc); decay.append(dec)
        x.append(xg); tinv.append(eye + xg)
    for _ in range(int(math.log2(c)) - 1):
        x = [_dot(xg.astype(BF16), xg.astype(BF16)) for xg in x]
        tinv = [tg + _dot(tg.astype(BF16), xg.astype(BF16)) for tg, xg in zip(tinv, x)]
    for gi, heads in enumerate(groups):
        ks, kb, gc, dec = k_st[gi], kb_st[gi], gc_col[gi], decay[gi]
        tb = tinv[gi].astype(BF16)
        bt = stack(lambda h: beta[:, h:h + 1], heads)
        vs = stack(lambda h: v_ref[:, h * G_DV:(h + 1) * G_DV], heads)
        qs = stack(lambda h: q_ref[:, (h // rep) * G_DK:(h // rep + 1) * G_DK], heads)
        eg = jnp.exp(gc)
        u_st = _dot(tb, (vs * bt).astype(BF16))
        w_st = _dot(tb, (kb * eg).astype(BF16))
        qg_st = qs * eg
        qk_ref[0, gi] = (_dot_nt(qs.astype(BF16), ks.astype(BF16)) * dec).astype(BF16)
        gl = stack(lambda h: jnp.broadcast_to(gcum[c - 1:c, h:h + 1], (c, 1)), heads)
        kdec = ks * jnp.exp(gl - gc)
        for tt in range(r // 128):
            kdt_ref[0, gi * (r // 128) + tt] = kdec[tt * 128:(tt + 1) * 128, :].T.astype(BF16)
        for hl, h in enumerate(heads):
            rows = slice(hl * c, (hl + 1) * c)
            u_ref[:, h * G_DV:(h + 1) * G_DV] = u_st[rows]
            wq_ref[0, h] = jnp.concatenate([w_st[rows], qg_st[rows]], axis=0).astype(BF16)


def _gdn_local(qn, kn, v, g, beta, c):
    n = qn.shape[0]
    nc = n // c
    hv = G_V_HEADS
    hpg, ngroups, _, ntiles = _gdn_geometry(c)
    r = hpg * c
    row = lambda i: (i, 0)
    blk4 = lambda i: (i, 0, 0, 0)
    return pl.pallas_call(
        functools.partial(_gdn_local_kernel, c=c),
        grid=(nc,),
        in_specs=[pl.BlockSpec((c, G_QK_DIM), row),
                  pl.BlockSpec((c, G_QK_DIM), row),
                  pl.BlockSpec((c, G_V_DIM), row),
                  pl.BlockSpec((c, hv), row),
                  pl.BlockSpec((c, hv), row)],
        out_specs=[pl.BlockSpec((c, G_V_DIM), row),
                   pl.BlockSpec((1, hv, 2 * c, G_DK), blk4),
                   pl.BlockSpec((1, ngroups, r, r), blk4),
                   pl.BlockSpec((1, ntiles, 128, 128), blk4),
                   pl.BlockSpec((1, 1, hv), lambda i: (i, 0, 0))],
        out_shape=[jax.ShapeDtypeStruct((n, G_V_DIM), F32),
                   jax.ShapeDtypeStruct((nc, hv, 2 * c, G_DK), BF16),
                   jax.ShapeDtypeStruct((nc, ngroups, r, r), BF16),
                   jax.ShapeDtypeStruct((nc, ntiles, 128, 128), BF16),
                   jax.ShapeDtypeStruct((nc, 1, hv), F32)],
        compiler_params=_cp(("arbitrary",), 48),
        name="gdn_local",
    )(qn, kn, v, g, beta)


def _gdn_scan_kernel(u_ref, wq_ref, qk_ref, kdt_ref, egl_ref, z_ref, s0_ref, nw_ref,
                     o_ref, sfin_ref, s_scr, *, c, nchunks):
    j = pl.program_id(1)

    @pl.when(j == 0)
    def _():
        s_scr[...] = s0_ref[0]

    hpg, ngroups, hpt, ntiles = _gdn_geometry(c)
    egl = egl_ref[0]
    nw = nw_ref[...]
    hv = G_V_HEADS
    s_all = [s_scr[h] for h in range(hv)]
    ws = [_dot(wq_ref[0, h], s_all[h].astype(BF16)) for h in range(hv)]
    vnew = [u_ref[:, h * G_DV:(h + 1) * G_DV] - ws[h][0:c] for h in range(hv)]
    for gi in range(ngroups):
        heads = range(gi * hpg, (gi + 1) * hpg)
        vn_st = jnp.concatenate([vnew[h] for h in heads], axis=0).astype(BF16)
        o_st = jnp.concatenate([ws[h][c:2 * c] for h in heads], axis=0) + _dot(qk_ref[0, gi], vn_st)
        for hl, h in enumerate(heads):
            o = o_st[hl * c:(hl + 1) * c]
            zf = z_ref[:, h * G_DV:(h + 1) * G_DV]
            o = o * lax.rsqrt(jnp.mean(o * o, axis=-1, keepdims=True) + RMS_EPS) * nw * _silu(zf)
            o_ref[:, h * G_DV:(h + 1) * G_DV] = o
    row_head = lax.broadcasted_iota(jnp.int32, (128, G_DV), 0) // c
    for tt in range(ntiles):
        heads = range(tt * hpt, (tt + 1) * hpt)
        vn_t = jnp.concatenate([vnew[h] for h in heads], axis=0)
        vbd = jnp.concatenate([jnp.where(row_head == hl, vn_t, 0.0) for hl in range(hpt)],
                              axis=1).astype(BF16)
        upd = _dot(kdt_ref[0, tt], vbd)
        for hl, h in enumerate(heads):
            s_scr[h] = s_all[h] * egl[:, h:h + 1] + upd[:, hl * G_DV:(hl + 1) * G_DV]

    @pl.when(j == nchunks - 1)
    def _():
        sfin_ref[0] = s_scr[...]


def _gdn_scan(u, wq, qk, kdt, egl, big, s0, nw, nseq, c):
    n = u.shape[0]
    t = n // nseq
    nchunks = t // c
    hv = G_V_HEADS
    hpg, ngroups, _, ntiles = _gdn_geometry(c)
    r = hpg * c
    row = lambda s, j: (s * nchunks + j, 0)
    blk4 = lambda s, j: (s * nchunks + j, 0, 0, 0)
    zcol = G_QKV_DIM // G_V_DIM
    return pl.pallas_call(
        functools.partial(_gdn_scan_kernel, c=c, nchunks=nchunks),
        grid=(nseq, nchunks),
        in_specs=[pl.BlockSpec((c, G_V_DIM), row),
                  pl.BlockSpec((1, hv, 2 * c, G_DK), blk4),
                  pl.BlockSpec((1, ngroups, r, r), blk4),
                  pl.BlockSpec((1, ntiles, 128, 128), blk4),
                  pl.BlockSpec((1, 1, hv), lambda s, j: (s * nchunks + j, 0, 0)),
                  pl.BlockSpec((c, G_V_DIM), lambda s, j: (s * nchunks + j, zcol)),
                  pl.BlockSpec((1, hv, G_DK, G_DV), lambda s, j: (s, 0, 0, 0)),
                  pl.BlockSpec((1, G_DV), lambda s, j: (0, 0))],
        out_specs=[pl.BlockSpec((c, G_V_DIM), row),
                   pl.BlockSpec((1, hv, G_DK, G_DV), lambda s, j: (s, 0, 0, 0))],
        out_shape=[jax.ShapeDtypeStruct((n, G_V_DIM), F32),
                   jax.ShapeDtypeStruct((nseq, hv, G_DK, G_DV), F32)],
        scratch_shapes=[pltpu.VMEM((hv, G_DK, G_DV), F32)],
        compiler_params=_cp(("arbitrary", "arbitrary"), 48),
        name="gdn_scan",
    )(u, wq, qk, kdt, egl, big, s0, nw)


def _trunk(x, nseq, pos, moba_fn, sconv_hist, gdn_s0, gdn_hist, mem_k, mem_v, wts, tiles):
    n = x.shape[0]
    t = n // nseq
    assert t >= G_CONV_W - 1 and t >= SCONV_W - 1
    row = lambda a: a.reshape(1, -1)
    ln_g, ln_b = wts["ln_g"], wts["ln_b"]

    def mem_moe(x, l):
        nmem = mem_k.shape[0] // DEPTH
        x = _memattn(x, wts["mem_wq"][l], mem_k, mem_v, wts["mem_wo"][l],
                     row(ln_g[l, 1]), row(ln_b[l, 1]), tiles["mem"], l * nmem, nmem)
        return _moe(x, wts["w_router"], row(wts["router_bias"]), wts["moe_wg"], wts["moe_wu"], wts["moe_wd"],
                    row(ln_g[l, 2]), row(ln_b[l, 2]), tiles["moe"], l)

    qb, kf, kb, vf, vb, cu, gb = _ab_in(x, wts["w_in_ab"], _rope_tables(pos), tiles["ab_in"])
    attn = moba_fn(qb, kf, kb, vb)
    x = _ab_out(attn, cu, sconv_hist, gb, x, wts["w_out_ab"], wts["sconv_w"],
                row(ln_g[0, 0]), row(ln_b[0, 0]), nseq, tiles["ab_out"])
    sconv_new = cu.reshape(nseq, t, SCONV_CH)[:, t - (SCONV_W - 1):]
    x = mem_moe(x, 0)

    big = _mm(x, wts["w_gdn_main"], tiles["mm"], 1536, "gdn_in")
    qn, kn, v, g, beta = _gdn_prep(big, gdn_hist, wts["gdn_conv_w"], x, wts["w_gdn_a"], wts["w_gdn_b"],
                                   row(wts["gdn_a_log"]), row(wts["gdn_dt_bias"]), nseq, tiles["gdn_prep"])
    c = min(G_CHUNK, t)
    u, wq, qk, kdt, egl = _gdn_local(qn, kn, v, g, beta, c)
    og, gdn_s = _gdn_scan(u, wq, qk, kdt, egl, big, gdn_s0, row(wts["gdn_norm_w"]), nseq, c)
    gdn_hist_new = big.reshape(nseq, t, -1)[:, t - (G_CONV_W - 1):, :G_QKV_DIM]
    x = _mm_res_ln(og, wts["w_out_gdn"], x, row(ln_g[1, 0]), row(ln_b[1, 0]), tiles["mm"], "gdn_out")
    x = mem_moe(x, 1)
    return x, kf, vf, sconv_new, gdn_s, gdn_hist_new


def kernel(x_prompt, x_sample, mem_prompt, cache_moba_k, cache_moba_v, page_table, state_sconv,
           state_gdn, state_gdn_conv, cache_mem_k, cache_mem_v, w_in_ab, w_out_ab, sconv_w, w_in_gdn,
           gdn_conv_w, gdn_a_log, gdn_dt_bias, gdn_norm_w, w_out_gdn, mem_wq, mem_wk, mem_wv, mem_wo,
           ln_g, ln_b, w_router, router_bias, moe_w_gate, moe_w_up, moe_w_down):
    b, seq, _ = x_prompt.shape
    db, dseq, _ = x_sample.shape
    assert b == 1 and seq % KBLK == 0
    zcols = G_QKV_DIM + G_V_DIM
    wts = dict(
        w_in_ab=w_in_ab.astype(BF16), w_out_ab=w_out_ab.astype(BF16), sconv_w=sconv_w,
        w_gdn_main=w_in_gdn[:, :zcols].astype(BF16),
        w_gdn_a=w_in_gdn[:, zcols:zcols + G_V_HEADS], w_gdn_b=w_in_gdn[:, zcols + G_V_HEADS:],
        gdn_conv_w=gdn_conv_w, gdn_a_log=gdn_a_log, gdn_dt_bias=gdn_dt_bias, gdn_norm_w=gdn_norm_w,
        w_out_gdn=w_out_gdn.astype(BF16),
        mem_wq=mem_wq.astype(BF16), mem_wo=mem_wo.astype(BF16),
        ln_g=ln_g, ln_b=ln_b, w_router=w_router, router_bias=router_bias,
        moe_wg=moe_w_gate, moe_wu=moe_w_up, moe_wd=moe_w_down)

    mw = MEM_HEADS * MEM_HD
    w_mem_kv = jnp.concatenate([mem_wk[l] for l in range(DEPTH)] + [mem_wv[l] for l in range(DEPTH)],
                               axis=1).astype(BF16)
    mem_kv = _mm(mem_prompt.reshape(MEM_LEN, D_MODEL), w_mem_kv, MEM_LEN, 1024, "mem_kv")
    mem_k_p = jnp.stack([mem_kv[:, l * mw:(l + 1) * mw] for l in range(DEPTH)]).reshape(DEPTH, b, MEM_LEN, mw)
    mem_v_p = jnp.stack([mem_kv[:, (DEPTH + l) * mw:(DEPTH + l + 1) * mw]
                         for l in range(DEPTH)]).reshape(DEPTH, b, MEM_LEN, mw)

    def moba_p(qb, kf, kb, vb):
        return _moba_prompt(qb, kb, vb, _block_mean(kf))

    tiles_p = dict(ab_in=512, ab_out=512, mem=512, moe=1024, mm=1024, gdn_prep=256)
    y_p, k_p, v_p, sconv_p, gdn_s_p, gdn_conv_p = _trunk(
        x_prompt.reshape(seq, D_MODEL), b, jnp.arange(seq, dtype=jnp.int32), moba_p,
        jnp.zeros((b, SCONV_W - 1, SCONV_CH), F32),
        jnp.zeros((b, G_V_HEADS, G_DK, G_DV), F32),
        jnp.zeros((b, G_CONV_W - 1, G_QKV_DIM), F32),
        mem_k_p.reshape(DEPTH * b, MEM_LEN, mw), mem_v_p.reshape(DEPTH * b, MEM_LEN, mw), wts, tiles_p)

    past_len = page_table.shape[1] * PAGE
    pos_s = jnp.tile(past_len + jnp.arange(dseq, dtype=jnp.int32), db)

    def moba_s(qb, kf, kb, vb):
        return _moba_sample(qb, kb, vb, cache_moba_k, cache_moba_v, page_table, dseq)

    def mem_stored(c):
        chunks = MEM_HD // 128
        c = c.reshape(DEPTH * db, MEM_LEN, MEM_HEADS, chunks, 128).transpose(0, 1, 3, 2, 4)
        return c.reshape(DEPTH * db, MEM_LEN * chunks * MEM_HEADS, 128)

    ns = db * dseq
    tiles_s = dict(ab_in=ns, ab_out=dseq, mem=dseq, moe=ns, mm=ns, gdn_prep=dseq)
    y_s, k_s, v_s, sconv_s, gdn_s_s, gdn_conv_s = _trunk(
        x_sample.reshape(ns, D_MODEL), db, pos_s, moba_s, state_sconv, state_gdn, state_gdn_conv,
        mem_stored(cache_mem_k), mem_stored(cache_mem_v), wts, tiles_s)

    hd = (HEADS_A, HD_A)
    return (y_p.reshape(b, seq, D_MODEL), y_s.reshape(db, dseq, D_MODEL),
            k_p.reshape(b, seq, *hd), v_p.reshape(b, seq, *hd),
            sconv_p, gdn_s_p, gdn_conv_p,
            mem_k_p.reshape(DEPTH, b, MEM_LEN, MEM_HEADS, MEM_HD),
            mem_v_p.reshape(DEPTH, b, MEM_LEN, MEM_HEADS, MEM_HD),
            k_s.reshape(db, dseq, *hd), v_s.reshape(db, dseq, *hd),
            sconv_s, gdn_s_s, gdn_conv_s)
```
